```python
import math
import jax, jax.numpy as jnp
from jax import lax
import numpy as np

D_MODEL = 1024
BATCH = 8
SEQ = 4096
DEPTH = 4

N_MIXERS = 3
Q_BLOCK = 128
EPS = 1e-6
D_FF = 4 * D_MODEL
MLA_HEADS = 16
MLA_NOPE = 64
MLA_ROPE = 32
MLA_V = 64
MLA_Q_RANK = 384
MLA_KV_RANK = 256
ROPE_THETA = 10000.0
CONV_WIDTH = 3
SB_HEADS = 16
SB_HEAD_DIM = D_MODEL // SB_HEADS

kernel_name = "hybrid_mla_shortconv_stickbreaking_trunk"


def rmsnorm(x, g):
    xf = x.astype(jnp.float32)
    y = xf * lax.rsqrt(jnp.mean(xf * xf, axis=-1, keepdims=True) + EPS)
    return (y * g.astype(jnp.float32)).astype(x.dtype)


def rope(x, ang):
    half = x.shape[-1] // 2
    x1, x2 = x[..., :half], x[..., half:]
    cos = jnp.cos(ang).astype(x.dtype)
    sin = jnp.sin(ang).astype(x.dtype)
    return jnp.concatenate([x1 * cos - x2 * sin, x2 * cos + x1 * sin], axis=-1)


def to_blocks(q):
    b, s, h, d = q.shape
    return q.reshape(b, s // Q_BLOCK, Q_BLOCK, h, d).transpose(1, 0, 3, 2, 4)


def from_blocks(o):
    nb, b, h, qb, d = o.shape
    return o.transpose(1, 0, 3, 2, 4).reshape(b, nb * qb, h * d)


def causal_softmax_attention(q, k, v, scale):
    s_len = q.shape[1]
    kt = k.transpose(0, 2, 1, 3)
    vt = v.transpose(0, 2, 1, 3)
    kpos = jnp.arange(s_len)

    def body(args):
        qb, bi = args
        qpos = bi * Q_BLOCK + jnp.arange(Q_BLOCK)
        sc = jnp.einsum('bhqd,bhkd->bhqk', qb, kt).astype(jnp.float32) * scale
        mask = kpos[None, :] <= qpos[:, None]
        p = jax.nn.softmax(jnp.where(mask, sc, -jnp.inf), axis=-1)
        return jnp.einsum('bhqk,bhkd->bhqd', p.astype(vt.dtype), vt)

    out = lax.map(body, (to_blocks(q), jnp.arange(s_len // Q_BLOCK)))
    return from_blocks(out)


def stick_breaking_attention(q, k, v, scale):
    s_len = q.shape[1]
    kt = k.transpose(0, 2, 1, 3)
    vt = v.transpose(0, 2, 1, 3)
    kpos = jnp.arange(s_len)

    def body(args):
        qb, bi = args
        qpos = bi * Q_BLOCK + jnp.arange(Q_BLOCK)
        z = jnp.einsum('bhqd,bhkd->bhqk', qb, kt).astype(jnp.float32) * scale
        strict = kpos[None, :] < qpos[:, None]
        log_1m_beta = jnp.where(strict, jax.nn.log_sigmoid(-z), 0.0)
        key_axis = log_1m_beta.ndim - 1
        suffix = lax.cumsum(log_1m_beta, axis=key_axis, reverse=True) - log_1m_beta
        log_a = jax.nn.log_sigmoid(z) + suffix
        a = jnp.where(strict, jnp.exp(log_a), 0.0)
        return jnp.einsum('bhqk,bhkd->bhqd', a.astype(vt.dtype), vt)

    out = lax.map(body, (to_blocks(q), jnp.arange(s_len // Q_BLOCK)))
    return from_blocks(out)


def mla_mixer(h, positions, w_dq, norm_q, w_uq, w_dkv, norm_kv, w_uk, w_uv, w_o):
    b, s, _ = h.shape
    inv_freq = ROPE_THETA ** (-jnp.arange(0, MLA_ROPE, 2, dtype=jnp.float32) / MLA_ROPE)
    ang = positions.astype(jnp.float32)[..., None] * inv_freq
    cq = rmsnorm(h @ w_dq, norm_q)
    q = (cq @ w_uq).reshape(b, s, MLA_HEADS, MLA_NOPE + MLA_ROPE)
    q_nope, q_rope = q[..., :MLA_NOPE], q[..., MLA_NOPE:]
    q_rope = rope(q_rope, ang[:, :, None, :])
    ckr = h @ w_dkv
    ckv = rmsnorm(ckr[..., :MLA_KV_RANK], norm_kv)
    k_rope = rope(ckr[..., MLA_KV_RANK:], ang)
    k_nope = (ckv @ w_uk).reshape(b, s, MLA_HEADS, MLA_NOPE)
    v = (ckv @ w_uv).reshape(b, s, MLA_HEADS, MLA_V)
    qf = jnp.concatenate([q_nope, q_rope], axis=-1)
    kf = jnp.concatenate([k_nope, jnp.broadcast_to(k_rope[:, :, None, :], (b, s, MLA_HEADS, MLA_ROPE))], axis=-1)
    o = causal_softmax_attention(qf, kf, v, 1.0 / math.sqrt(MLA_NOPE + MLA_ROPE))
    return o @ w_o


def conv_mixer(h, w_in, conv_w, conv_b, w_out):
    bcu = h @ w_in
    gb, gc, u = bcu[..., :D_MODEL], bcu[..., D_MODEL:2 * D_MODEL], bcu[..., 2 * D_MODEL:]
    u = gc * u
    up = jnp.pad(u, ((0, 0), (CONV_WIDTH - 1, 0), (0, 0)))
    s = u.shape[1]
    y = conv_w[0] * up[:, 0:s] + conv_w[1] * up[:, 1:s + 1] + conv_w[2] * up[:, 2:s + 2] + conv_b
    return (gb * y) @ w_out


def sb_mixer(h, w_qkv, w_o):
    b, s, _ = h.shape
    qkv = (h @ w_qkv).reshape(b, s, 3, SB_HEADS, SB_HEAD_DIM)
    q, k, v = qkv[:, :, 0], qkv[:, :, 1], qkv[:, :, 2]
    o = stick_breaking_attention(q, k, v, 1.0 / math.sqrt(SB_HEAD_DIM))
    return o @ w_o


def sqrelu_mlp(h, w_up, w_down):
    return jnp.square(jax.nn.relu(h @ w_up)) @ w_down


def setup_inputs(seed: int = 0) -> dict:
    key = jax.random.key(seed)
    keys = iter(jax.random.split(key, 64))

    def dense(shape, extra=1.0):
        return jax.random.normal(next(keys), shape, jnp.float32) * (shape[0] ** -0.5) * extra

    def gain(n):
        return 1.0 + 0.01 * jax.random.normal(next(keys), (n,), jnp.float32)

    p = {}
    p["x"] = jax.random.normal(next(keys), (BATCH, SEQ, D_MODEL), jnp.float32)
    p["positions"] = jnp.broadcast_to(jnp.arange(SEQ, dtype=jnp.int32)[None, :], (BATCH, SEQ))
    out_scale = (2.0 * DEPTH) ** -0.5
    for i in range(DEPTH):
        kind = i % N_MIXERS
        pre = f"l{i}_"
        p[pre + "norm_mix"] = gain(D_MODEL)
        if kind == 0:
            p[pre + "w_dq"] = dense((D_MODEL, MLA_Q_RANK))
            p[pre + "norm_q"] = gain(MLA_Q_RANK)
            p[pre + "w_uq"] = dense((MLA_Q_RANK, MLA_HEADS * (MLA_NOPE + MLA_ROPE)))
            p[pre + "w_dkv"] = dense((D_MODEL, MLA_KV_RANK + MLA_ROPE))
            p[pre + "norm_kv"] = gain(MLA_KV_RANK)
            p[pre + "w_uk"] = dense((MLA_KV_RANK, MLA_HEADS * MLA_NOPE))
            p[pre + "w_uv"] = dense((MLA_KV_RANK, MLA_HEADS * MLA_V))
            p[pre + "w_o"] = dense((MLA_HEADS * MLA_V, D_MODEL), out_scale)
        elif kind == 1:
            p[pre + "w_in"] = dense((D_MODEL, 3 * D_MODEL))
            p[pre + "conv_w"] = jax.random.normal(next(keys), (CONV_WIDTH, D_MODEL), jnp.float32) * CONV_WIDTH ** -0.5
            p[pre + "conv_b"] = 0.01 * jax.random.normal(next(keys), (D_MODEL,), jnp.float32)
            p[pre + "w_out"] = dense((D_MODEL, D_MODEL), out_scale)
        else:
            p[pre + "w_qkv"] = dense((D_MODEL, 3 * SB_HEADS * SB_HEAD_DIM))
            p[pre + "w_o"] = dense((SB_HEADS * SB_HEAD_DIM, D_MODEL), out_scale)
        p[pre + "norm_mlp"] = gain(D_MODEL)
        p[pre + "w_up"] = dense((D_MODEL, D_FF))
        p[pre + "w_down"] = dense((D_FF, D_MODEL), out_scale)
    p["final_norm"] = gain(D_MODEL)
    return p


def reference(x, positions,
              l0_norm_mix, l0_w_dq, l0_norm_q, l0_w_uq, l0_w_dkv, l0_norm_kv, l0_w_uk, l0_w_uv, l0_w_o,
              l0_norm_mlp, l0_w_up, l0_w_down,
              l1_norm_mix, l1_w_in, l1_conv_w, l1_conv_b, l1_w_out,
              l1_norm_mlp, l1_w_up, l1_w_down,
              l2_norm_mix, l2_w_qkv, l2_w_o,
              l2_norm_mlp, l2_w_up, l2_w_down,
              l3_norm_mix, l3_w_dq, l3_norm_q, l3_w_uq, l3_w_dkv, l3_norm_kv, l3_w_uk, l3_w_uv, l3_w_o,
              l3_norm_mlp, l3_w_up, l3_w_down,
              final_norm):
    mixers = [
        lambda h: mla_mixer(h, positions, l0_w_dq, l0_norm_q, l0_w_uq, l0_w_dkv, l0_norm_kv, l0_w_uk, l0_w_uv, l0_w_o),
        lambda h: conv_mixer(h, l1_w_in, l1_conv_w, l1_conv_b, l1_w_out),
        lambda h: sb_mixer(h, l2_w_qkv, l2_w_o),
        lambda h: mla_mixer(h, positions, l3_w_dq, l3_norm_q, l3_w_uq, l3_w_dkv, l3_norm_kv, l3_w_uk, l3_w_uv, l3_w_o),
    ]
    mix_norms = [l0_norm_mix, l1_norm_mix, l2_norm_mix, l3_norm_mix]
    mlps = [(l0_norm_mlp, l0_w_up, l0_w_down), (l1_norm_mlp, l1_w_up, l1_w_down),
            (l2_norm_mlp, l2_w_up, l2_w_down), (l3_norm_mlp, l3_w_up, l3_w_down)]
    for i in range(DEPTH):
        x = x + mixers[i](rmsnorm(x, mix_norms[i]))
        g, w_up, w_down = mlps[i]
        x = x + sqrelu_mlp(rmsnorm(x, g), w_up, w_down)
    return rmsnorm(x, final_norm)
```

```python
import functools
import math

import jax
import jax.numpy as jnp
from jax import lax
from jax.experimental import pallas as pl
from jax.experimental.pallas import tpu as pltpu

F32 = jnp.float32
BF16 = jnp.bfloat16

EPS = 1e-6
ROPE_THETA = 10000.0
MLA_HEADS = 16
MLA_NOPE = 64
MLA_ROPE = 32
MLA_V = 64
MLA_Q_RANK = 384
MLA_KV_RANK = 256
SB_HEADS = 16

LANES = 128
HEAD_SLAB = LANES
VMEM_LIMIT = 56 * 1024 * 1024

TM_PROJ = 512
TQ_MLA = 512
TQ_SB = 256
FF_CHUNK = 1024


def _params(*sem):
    return pltpu.CompilerParams(dimension_semantics=sem, vmem_limit_bytes=VMEM_LIMIT)


def _resident(shape):
    nd = len(shape)
    return pl.BlockSpec(shape, lambda *_: (0,) * nd, pipeline_mode=pl.Buffered(1))


def _rms(xf, g):
    ms = jnp.mean(xf * xf, axis=-1, keepdims=True)
    return xf * lax.rsqrt(ms + EPS) * g


def _dot(a, b):
    return jnp.dot(a, b, preferred_element_type=F32)


def _dot_nt(a, b):
    return lax.dot_general(a, b, (((1,), (1,)), ((), ())), preferred_element_type=F32)


def _mla_proj_kernel(x_ref, pos_ref, g_ref, wd_ref, nq_ref, wuq_ref, nkv_ref, wuk_ref, wuv_ref,
                     invf_ref, q_ref, k_ref, v_ref, *, scale):
    h = _rms(x_ref[0], g_ref[...]).astype(BF16)
    d = _dot(h, wd_ref[...])
    cq = _rms(d[:, :MLA_Q_RANK], nq_ref[...]).astype(BF16)
    ckv = _rms(d[:, MLA_Q_RANK:MLA_Q_RANK + MLA_KV_RANK], nkv_ref[...]).astype(BF16)
    kr = d[:, MLA_Q_RANK + MLA_KV_RANK:]

    ang = pos_ref[0].astype(F32) * invf_ref[...]
    lane = lax.broadcasted_iota(jnp.int32, (1, LANES), 1)
    first_half = lane < MLA_NOPE + MLA_ROPE // 2
    cos = jnp.cos(ang)
    sin = jnp.sin(ang)
    sin = jnp.where(first_half, -sin, sin)

    def rotate(t, c, s):
        half = MLA_ROPE // 2
        partner = jnp.where(first_half, pltpu.roll(t, LANES - half, 1), pltpu.roll(t, half, 1))
        return t * c + partner * s

    kr = rotate(kr, cos, sin)
    cq_s = cos * scale
    sq_s = sin * scale
    qf = _dot(cq, wuq_ref[...])
    kf = _dot(ckv, wuk_ref[...])
    for hd in range(MLA_HEADS):
        sl = slice(hd * HEAD_SLAB, (hd + 1) * HEAD_SLAB)
        q_ref[0, :, sl] = rotate(qf[:, sl], cq_s, sq_s).astype(BF16)
        k_ref[0, :, sl] = (kf[:, sl] + kr).astype(BF16)
    v_ref[0] = _dot(ckv, wuv_ref[...]).astype(BF16)


def _mla_proj(x, pos, g, wd, nq, wuq, nkv, wuk, wuv, invf):
    b, s, dm = x.shape
    tm = TM_PROJ
    hw = MLA_HEADS * HEAD_SLAB
    tok = lambda w: pl.BlockSpec((1, tm, w), lambda bi, si: (bi, si, 0))
    return pl.pallas_call(
        functools.partial(_mla_proj_kernel, scale=1.0 / math.sqrt(MLA_NOPE + MLA_ROPE)),
        grid=(b, s // tm),
        in_specs=[tok(dm), tok(1), _resident(g.shape), _resident(wd.shape), _resident(nq.shape),
                  _resident(wuq.shape), _resident(nkv.shape), _resident(wuk.shape),
                  _resident(wuv.shape), _resident(invf.shape)],
        out_specs=[tok(hw), tok(hw), tok(MLA_HEADS * MLA_V)],
        out_shape=[jax.ShapeDtypeStruct((b, s, hw), BF16), jax.ShapeDtypeStruct((b, s, hw), BF16),
                   jax.ShapeDtypeStruct((b, s, MLA_HEADS * MLA_V), BF16)],
        compiler_params=_params("arbitrary", "arbitrary"),
        name="mla_proj",
    )(x, pos, g, wd, nq, wuq, nkv, wuk, wuv, invf)


def _mla_attn_kernel(q_ref, k_ref, v_ref, o_ref, m_scr, l_scr, acc_scr, *, tq):
    qi = pl.program_id(2)
    lane = lax.broadcasted_iota(jnp.int32, (1, LANES), 1)
    low = lane < MLA_V
    row = lax.broadcasted_iota(jnp.int32, (tq, tq), 0)
    col = lax.broadcasted_iota(jnp.int32, (tq, tq), 1)
    q = q_ref[0]
    m_scr[...] = jnp.full(m_scr.shape, -jnp.inf, F32)
    l_scr[...] = jnp.zeros(l_scr.shape, F32)
    acc_scr[...] = jnp.zeros(acc_scr.shape, F32)

    def block(j, diagonal):
        ks = pl.multiple_of(j * tq, tq)
        kb = k_ref[0, pl.ds(ks, tq), :]
        vb = v_ref[0, pl.ds(ks, tq), :]
        pv = None
        alphas = []
        for hd in range(2):
            sl = slice(hd * HEAD_SLAB, (hd + 1) * HEAD_SLAB)
            s = _dot_nt(q[:, sl], kb[:, sl])
            if diagonal:
                s = jnp.where(col <= row, s, -jnp.inf)
            m_old = m_scr[hd]
            m_new = jnp.maximum(m_old, jnp.max(s, axis=1, keepdims=True))
            alpha = jnp.exp(m_old - m_new)
            p = jnp.exp(s - m_new)
            l_scr[hd] = alpha * l_scr[hd] + jnp.sum(p, axis=1, keepdims=True)
            m_scr[hd] = m_new
            vm = jnp.where(low if hd == 0 else jnp.logical_not(low), vb, jnp.zeros_like(vb))
            part = _dot(p.astype(BF16), vm)
            pv = part if pv is None else pv + part
            alphas.append(alpha)
        acc_scr[...] = acc_scr[...] * jnp.where(low, alphas[0], alphas[1]) + pv

    def body(j, carry):
        block(j, False)
        return carry

    lax.fori_loop(0, qi, body, 0)
    block(qi, True)
    inv_l = jnp.where(low, 1.0 / l_scr[0], 1.0 / l_scr[1])
    o_ref[0] = (acc_scr[...] * inv_l).astype(BF16)


def _mla_attn(q, k, v):
    b, s, _ = q.shape
    tq = TQ_MLA
    pairs = MLA_HEADS // 2
    return pl.pallas_call(
        functools.partial(_mla_attn_kernel, tq=tq),
        grid=(b, pairs, s // tq),
        in_specs=[pl.BlockSpec((1, tq, 2 * HEAD_SLAB), lambda bi, pi, qi: (bi, qi, pi)),
                  pl.BlockSpec((1, s, 2 * HEAD_SLAB), lambda bi, pi, qi: (bi, 0, pi)),
                  pl.BlockSpec((1, s, LANES), lambda bi, pi, qi: (bi, 0, pi))],
        out_specs=pl.BlockSpec((1, tq, LANES), lambda bi, pi, qi: (bi, qi, pi)),
        out_shape=jax.ShapeDtypeStruct((b, s, MLA_HEADS * MLA_V), BF16),
        scratch_shapes=[pltpu.VMEM((2, tq, 1), F32), pltpu.VMEM((2, tq, 1), F32),
                        pltpu.VMEM((tq, LANES), F32)],
        compiler_params=_params("arbitrary", "arbitrary", "arbitrary"),
        name="mla_attn",
    )(q, k, v)


def _conv_kernel(x_ref, g_ref, win_ref, cw_ref, cb_ref, o_ref, tail_scr, *, tm, dm):
    @pl.when(pl.program_id(1) == 0)
    def _():
        tail_scr[...] = jnp.zeros(tail_scr.shape, F32)

    h = _rms(x_ref[0], g_ref[...]).astype(BF16)
    bcu = _dot(h, win_ref[...])
    gb = bcu[:, :dm]
    u = bcu[:, dm:2 * dm] * bcu[:, 2 * dm:]
    tail = tail_scr[...]
    row = lax.broadcasted_iota(jnp.int32, (tm, 1), 0)
    prev1 = jnp.where(row == 0, tail[7:8], pltpu.roll(u, 1, 0))
    prev2 = jnp.where(row == 0, tail[6:7], jnp.where(row == 1, tail[7:8], pltpu.roll(u, 2, 0)))
    y = cw_ref[0:1] * prev2 + cw_ref[1:2] * prev1 + cw_ref[2:3] * u + cb_ref[...]
    o_ref[0] = (gb * y).astype(BF16)
    tail_scr[...] = u[tm - 8:, :]


def _conv_pre(x, g, win, cw, cb):
    b, s, dm = x.shape
    tm = TM_PROJ
    tok = pl.BlockSpec((1, tm, dm), lambda bi, si: (bi, si, 0))
    return pl.pallas_call(
        functools.partial(_conv_kernel, tm=tm, dm=dm),
        grid=(b, s // tm),
        in_specs=[tok, _resident(g.shape), _resident(win.shape), _resident(cw.shape),
                  _resident(cb.shape)],
        out_specs=tok,
        out_shape=jax.ShapeDtypeStruct((b, s, dm), BF16),
        scratch_shapes=[pltpu.VMEM((8, dm), F32)],
        compiler_params=_params("arbitrary", "arbitrary"),
        name="conv_pre",
    )(x, g, win, cw, cb)


def _sb_proj_kernel(x_ref, g_ref, w_ref, o_ref, *, dm, scale):
    h = _rms(x_ref[...], g_ref[...]).astype(BF16)
    r = _dot(h, w_ref[...])
    o_ref[:, :dm] = (r[:, :dm] * scale).astype(BF16)
    o_ref[:, dm:] = r[:, dm:].astype(BF16)


def _sb_proj(x2, g, w, scale):
    t, dm = x2.shape
    tm = TM_PROJ
    return pl.pallas_call(
        functools.partial(_sb_proj_kernel, dm=dm, scale=scale),
        grid=(t // tm,),
        in_specs=[pl.BlockSpec((tm, dm), lambda i: (i, 0)), _resident(g.shape), _resident(w.shape)],
        out_specs=pl.BlockSpec((tm, 3 * dm), lambda i: (i, 0)),
        out_shape=jax.ShapeDtypeStruct((t, 3 * dm), BF16),
        compiler_params=_params("arbitrary"),
        name="sb_proj",
    )(x2, g, w)


def _sb_attn_kernel(q_ref, k_ref, v_ref, tri_ref, o_ref, run_scr, acc_scr, *, tq):
    qi = pl.program_id(2)
    lane = lax.broadcasted_iota(jnp.int32, (1, LANES), 1)
    low = lane < LANES // 2
    row = lax.broadcasted_iota(jnp.int32, (tq, tq), 0)
    col = lax.broadcasted_iota(jnp.int32, (tq, tq), 1)
    q = q_ref[0]
    zero_q = jnp.zeros_like(q)
    q_heads = [jnp.where(low, q, zero_q), jnp.where(low, zero_q, q)]
    tri = tri_ref[...]
    run_scr[...] = jnp.zeros(run_scr.shape, F32)
    acc_scr[...] = jnp.zeros(acc_scr.shape, F32)

    def block(j, diagonal):
        ks = pl.multiple_of(j * tq, tq)
        kb = k_ref[0, pl.ds(ks, tq), :]
        vb = v_ref[0, pl.ds(ks, tq), :]
        pv = None
        for hd in range(2):
            z = _dot_nt(q_heads[hd], kb)
            softplus = jnp.maximum(z, 0.0) + jnp.log1p(jnp.exp(-jnp.abs(z)))
            log_1m = -softplus
            if diagonal:
                log_1m = jnp.where(col < row, log_1m, 0.0)
            hi = log_1m.astype(BF16)
            lo = (log_1m - hi.astype(F32)).astype(BF16)
            run = run_scr[hd]
            suffix = _dot(hi, tri) + _dot(lo, tri) + run
            a = jnp.exp((z - softplus) + suffix)
            if diagonal:
                a = jnp.where(col < row, a, 0.0)
            run_scr[hd] = run + jnp.sum(log_1m, axis=1, keepdims=True)
            vm = jnp.where(low if hd == 0 else jnp.logical_not(low), vb, jnp.zeros_like(vb))
            part = _dot(a.astype(BF16), vm)
            pv = part if pv is None else pv + part
        acc_scr[...] += pv

    block(qi, True)

    def body(t, carry):
        block(qi - 1 - t, False)
        return carry

    lax.fori_loop(0, qi, body, 0)
    o_ref[0] = acc_scr[...].astype(BF16)


def _sb_attn(qkv, tri):
    b, s, w3 = qkv.shape
    tq = TQ_SB
    pairs = SB_HEADS // 2
    return pl.pallas_call(
        functools.partial(_sb_attn_kernel, tq=tq),
        grid=(b, pairs, s // tq),
        in_specs=[pl.BlockSpec((1, tq, LANES), lambda bi, pi, qi: (bi, qi, pi)),
                  pl.BlockSpec((1, s, LANES), lambda bi, pi, qi: (bi, 0, pairs + pi)),
                  pl.BlockSpec((1, s, LANES), lambda bi, pi, qi: (bi, 0, 2 * pairs + pi)),
                  _resident(tri.shape)],
        out_specs=pl.BlockSpec((1, tq, LANES), lambda bi, pi, qi: (bi, qi, pi)),
        out_shape=jax.ShapeDtypeStruct((b, s, w3 // 3), BF16),
        scratch_shapes=[pltpu.VMEM((2, tq, 1), F32), pltpu.VMEM((tq, LANES), F32)],
        compiler_params=_params("arbitrary", "arbitrary", "arbitrary"),
        name="sb_attn",
    )(qkv, qkv, qkv, tri)


def _mlp_kernel(x_ref, a_ref, wa_ref, g_ref, wup_ref, wdn_ref, gf_ref, o_ref, *, d_ff, final):
    x1 = x_ref[...] + _dot(a_ref[...], wa_ref[...])
    h = _rms(x1, g_ref[...]).astype(BF16)
    acc = x1
    for c in range(d_ff // FF_CHUNK):
        sl = slice(c * FF_CHUNK, (c + 1) * FF_CHUNK)
        r = jnp.maximum(_dot(h, wup_ref[:, sl]), 0.0)
        acc = acc + _dot((r * r).astype(BF16), wdn_ref[sl, :])
    if final:
        acc = _rms(acc, gf_ref[...])
    o_ref[...] = acc


def _mlp(x2, a2, wa, g, wup, wdn, gf, final):
    t, dm = x2.shape
    tm = TM_PROJ
    tok = pl.BlockSpec((tm, dm), lambda i: (i, 0))
    return pl.pallas_call(
        functools.partial(_mlp_kernel, d_ff=wup.shape[1], final=final),
        grid=(t // tm,),
        in_specs=[tok, tok, _resident(wa.shape), _resident(g.shape), _resident(wup.shape),
                  _resident(wdn.shape), _resident(gf.shape)],
        out_specs=tok,
        out_shape=jax.ShapeDtypeStruct((t, dm), F32),
        compiler_params=_params("arbitrary"),
        name="out_proj_mlp",
    )(x2, a2, wa, g, wup, wdn, gf)


def _row(v):
    return v.reshape(1, -1).astype(F32)


def _pad_heads(w, heads, width):
    k = w.shape[0]
    w = w.reshape(k, heads, width)
    return jnp.pad(w, ((0, 0), (0, 0), (0, HEAD_SLAB - width))).reshape(k, heads * HEAD_SLAB)


def _mla_layer(x, pos, invf, norm_mix, w_dq, norm_q, w_uq, w_dkv, norm_kv, w_uk, w_uv):
    w_kr = jnp.pad(w_dkv[:, MLA_KV_RANK:], ((0, 0), (MLA_NOPE, HEAD_SLAB - MLA_NOPE - MLA_ROPE)))
    wd = jnp.concatenate([w_dq, w_dkv[:, :MLA_KV_RANK], w_kr], axis=1).astype(BF16)
    wuq = _pad_heads(w_uq, MLA_HEADS, MLA_NOPE + MLA_ROPE).astype(BF16)
    wuk = _pad_heads(w_uk, MLA_HEADS, MLA_NOPE).astype(BF16)
    q, k, v = _mla_proj(x, pos, _row(norm_mix), wd, _row(norm_q), wuq, _row(norm_kv), wuk,
                        w_uv.astype(BF16), invf)
    return _mla_attn(q, k, v)


def kernel(x, positions, l0_norm_mix, l0_w_dq, l0_norm_q, l0_w_uq, l0_w_dkv, l0_norm_kv, l0_w_uk, l0_w_uv, l0_w_o, l0_norm_mlp, l0_w_up, l0_w_down, l1_norm_mix, l1_w_in, l1_conv_w, l1_conv_b, l1_w_out, l1_norm_mlp, l1_w_up, l1_w_down, l2_norm_mix, l2_w_qkv, l2_w_o, l2_norm_mlp, l2_w_up, l2_w_down, l3_norm_mix, l3_w_dq, l3_norm_q, l3_w_uq, l3_w_dkv, l3_norm_kv, l3_w_uk, l3_w_uv, l3_w_o, l3_norm_mlp, l3_w_up, l3_w_down, final_norm):
    b, s, dm = x.shape
    t = b * s
    pos = positions.reshape(b, s, 1)
    inv_freq = ROPE_THETA ** (-jnp.arange(0, MLA_ROPE, 2, dtype=F32) / MLA_ROPE)
    invf = jnp.zeros((1, LANES), F32)
    invf = invf.at[0, MLA_NOPE:MLA_NOPE + MLA_ROPE].set(jnp.concatenate([inv_freq, inv_freq]))
    idx = jnp.arange(TQ_SB)
    tri = (idx[:, None] > idx[None, :]).astype(BF16)
    gf = _row(final_norm)

    def mlp(xc, a, wa, g, wup, wdn, final=False):
        out = _mlp(xc.reshape(t, dm), a.reshape(t, dm), wa.astype(BF16), _row(g), wup.astype(BF16),
                   wdn.astype(BF16), gf, final)
        return out.reshape(b, s, dm)

    o = _mla_layer(x, pos, invf, l0_norm_mix, l0_w_dq, l0_norm_q, l0_w_uq, l0_w_dkv, l0_norm_kv,
                   l0_w_uk, l0_w_uv)
    x = mlp(x, o, l0_w_o, l0_norm_mlp, l0_w_up, l0_w_down)

    gy = _conv_pre(x, _row(l1_norm_mix), l1_w_in.astype(BF16), l1_conv_w.astype(F32), _row(l1_conv_b))
    x = mlp(x, gy, l1_w_out, l1_norm_mlp, l1_w_up, l1_w_down)

    qkv = _sb_proj(x.reshape(t, dm), _row(l2_norm_mix), l2_w_qkv.astype(BF16),
                   1.0 / math.sqrt(dm // SB_HEADS))
    o = _sb_attn(qkv.reshape(b, s, 3 * dm), tri)
    x = mlp(x, o, l2_w_o, l2_norm_mlp, l2_w_up, l2_w_down)

    o = _mla_layer(x, pos, invf, l3_norm_mix, l3_w_dq, l3_norm_q, l3_w_uq, l3_w_dkv, l3_norm_kv,
                   l3_w_uk, l3_w_uv)
    return mlp(x, o, l3_w_o, l3_norm_mlp, l3_w_up, l3_w_down, final=True)
```

```python
import functools
import math

import jax
import jax.numpy as jnp
from jax import lax
from jax.experimental import pallas as pl
from jax.experimental.pallas import tpu as pltpu

F32 = jnp.float32
BF16 = jnp.bfloat16

EPS = 1e-6
ROPE_THETA = 10000.0
MLA_HEADS = 16
MLA_NOPE = 64
MLA_ROPE = 32
MLA_V = 64
MLA_Q_RANK = 384
MLA_KV_RANK = 256
SB_HEADS = 16

LANES = 128
HEAD_SLAB = LANES
VMEM_LIMIT = 56 * 1024 * 1024

TM_PROJ = 512
TQ_MLA = 512
TQ_SB = 512
TK_SB = 256
LOG2_E = math.log2(math.e)
FF_CHUNK = 1024


def _params(*sem):
    return pltpu.CompilerParams(dimension_semantics=sem, vmem_limit_bytes=VMEM_LIMIT)


def _resident(shape):
    nd = len(shape)
    return pl.BlockSpec(shape, lambda *_: (0,) * nd, pipeline_mode=pl.Buffered(1))


def _rms(xf, g):
    ms = jnp.mean(xf * xf, axis=-1, keepdims=True)
    return xf * lax.rsqrt(ms + EPS) * g


def _dot(a, b):
    return jnp.dot(a, b, preferred_element_type=F32)


def _dot_nt(a, b):
    return lax.dot_general(a, b, (((1,), (1,)), ((), ())), preferred_element_type=F32)


def _mla_proj_kernel(x_ref, pos_ref, g_ref, wd_ref, nq_ref, wuq_ref, nkv_ref, wuk_ref, wuvt_ref,
                     invf_ref, q_ref, k_ref, vt_ref, *, scale):
    h = _rms(x_ref[0], g_ref[...]).astype(BF16)
    d = _dot(h, wd_ref[...])
    cq = _rms(d[:, :MLA_Q_RANK], nq_ref[...]).astype(BF16)
    ckv = _rms(d[:, MLA_Q_RANK:MLA_Q_RANK + MLA_KV_RANK], nkv_ref[...]).astype(BF16)
    kr = d[:, MLA_Q_RANK + MLA_KV_RANK:]

    ang = pos_ref[0].astype(F32) * invf_ref[...]
    lane = lax.broadcasted_iota(jnp.int32, (1, LANES), 1)
    first_half = lane < MLA_NOPE + MLA_ROPE // 2
    cos = jnp.cos(ang)
    sin = jnp.sin(ang)
    sin = jnp.where(first_half, -sin, sin)

    def rotate(t, c, s):
        half = MLA_ROPE // 2
        partner = jnp.where(first_half, pltpu.roll(t, LANES - half, 1), pltpu.roll(t, half, 1))
        return t * c + partner * s

    kr = rotate(kr, cos, sin)
    cq_s = cos * scale
    sq_s = sin * scale
    qf = _dot(cq, wuq_ref[...])
    kf = _dot(ckv, wuk_ref[...])
    for hd in range(MLA_HEADS):
        sl = slice(hd * HEAD_SLAB, (hd + 1) * HEAD_SLAB)
        q_ref[0, :, sl] = rotate(qf[:, sl], cq_s, sq_s).astype(BF16)
        k_ref[0, :, sl] = (kf[:, sl] + kr).astype(BF16)
    vt_ref[0] = _dot_nt(wuvt_ref[...], ckv).astype(BF16)


def _mla_proj(x, pos, g, wd, nq, wuq, nkv, wuk, wuvt, invf):
    b, s, dm = x.shape
    tm = TM_PROJ
    hw = MLA_HEADS * HEAD_SLAB
    vw = MLA_HEADS * MLA_V
    tok = lambda w: pl.BlockSpec((1, tm, w), lambda bi, si: (bi, si, 0))
    return pl.pallas_call(
        functools.partial(_mla_proj_kernel, scale=math.log2(math.e) / math.sqrt(MLA_NOPE + MLA_ROPE)),
        grid=(b, s // tm),
        in_specs=[tok(dm), tok(1), _resident(g.shape), _resident(wd.shape), _resident(nq.shape),
                  _resident(wuq.shape), _resident(nkv.shape), _resident(wuk.shape),
                  _resident(wuvt.shape), _resident(invf.shape)],
        out_specs=[tok(hw), tok(hw), pl.BlockSpec((1, vw, tm), lambda bi, si: (bi, 0, si))],
        out_shape=[jax.ShapeDtypeStruct((b, s, hw), BF16), jax.ShapeDtypeStruct((b, s, hw), BF16),
                   jax.ShapeDtypeStruct((b, vw, s), BF16)],
        compiler_params=_params("arbitrary", "arbitrary"),
        name="mla_proj",
    )(x, pos, g, wd, nq, wuq, nkv, wuk, wuvt, invf)


def _mla_attn_kernel(q_ref, k_ref, vt_ref, o_ref, m_scr, l_scr, acc_scr, sa_scr, sb_scr, *,
                     tq):
    qi = pl.program_id(2)
    q = q_ref[0]
    m_scr[...] = jnp.full(m_scr.shape, -jnp.inf, F32)
    l_scr[...] = jnp.zeros(l_scr.shape, F32)
    acc_scr[...] = jnp.zeros(acc_scr.shape, F32)

    def scores(j, s_scr):
        ks = pl.multiple_of(j * tq, tq)
        kb = k_ref[0, pl.ds(ks, tq), :]
        for hd in range(2):
            sl = slice(hd * HEAD_SLAB, (hd + 1) * HEAD_SLAB)
            s_scr[hd] = _dot_nt(kb[:, sl], q[:, sl])

    def finish(j, s_scr, diagonal):
        ks = pl.multiple_of(j * tq, tq)
        vtb = vt_ref[0, :, pl.ds(ks, tq)]
        for hd in range(2):
            rows = slice(hd * MLA_V, (hd + 1) * MLA_V)
            st = s_scr[hd]
            if diagonal:
                key = lax.broadcasted_iota(jnp.int32, st.shape, 0)
                qry = lax.broadcasted_iota(jnp.int32, st.shape, 1)
                st = jnp.where(key <= qry, st, -jnp.inf)
            m_old = m_scr[hd]
            m_new = jnp.maximum(m_old, jnp.max(st, axis=0, keepdims=True))
            alpha = jnp.exp2(m_old - m_new)
            pt = jnp.exp2(st - m_new)
            l_scr[hd] = alpha * l_scr[hd] + jnp.sum(pt, axis=0, keepdims=True)
            m_scr[hd] = m_new
            acc_scr[rows, :] = acc_scr[rows, :] * alpha + _dot(vtb[rows, :], pt.astype(BF16))

    scores(0, sa_scr)

    def body(t, carry):
        scores(2 * t + 1, sb_scr)
        finish(2 * t, sa_scr, False)
        scores(2 * t + 2, sa_scr)
        finish(2 * t + 1, sb_scr, False)
        return carry

    lax.fori_loop(0, qi // 2, body, 0)

    @pl.when(qi % 2 == 0)
    def _():
        finish(qi, sa_scr, True)

    @pl.when(qi % 2 == 1)
    def _():
        scores(qi, sb_scr)
        finish(qi - 1, sa_scr, False)
        finish(qi, sb_scr, True)

    for hd in range(2):
        rows = slice(hd * MLA_V, (hd + 1) * MLA_V)
        acc_scr[rows, :] = acc_scr[rows, :] * (1.0 / l_scr[hd])
    o_ref[0] = jnp.transpose(acc_scr[...]).astype(BF16)


def _mla_attn(q, k, vt):
    b, s, _ = q.shape
    tq = TQ_MLA
    pairs = MLA_HEADS // 2
    return pl.pallas_call(
        functools.partial(_mla_attn_kernel, tq=tq),
        grid=(b, pairs, s // tq),
        in_specs=[pl.BlockSpec((1, tq, 2 * HEAD_SLAB), lambda bi, pi, qi: (bi, qi, pi)),
                  pl.BlockSpec((1, s, 2 * HEAD_SLAB), lambda bi, pi, qi: (bi, 0, pi)),
                  pl.BlockSpec((1, 2 * MLA_V, s), lambda bi, pi, qi: (bi, pi, 0))],
        out_specs=pl.BlockSpec((1, tq, 2 * MLA_V), lambda bi, pi, qi: (bi, qi, pi)),
        out_shape=jax.ShapeDtypeStruct((b, s, MLA_HEADS * MLA_V), BF16),
        scratch_shapes=[pltpu.VMEM((2, 1, tq), F32), pltpu.VMEM((2, 1, tq), F32),
                        pltpu.VMEM((2 * MLA_V, tq), F32),
                        pltpu.VMEM((2, tq, tq), F32), pltpu.VMEM((2, tq, tq), F32)],
        compiler_params=_params("arbitrary", "arbitrary", "arbitrary"),
        name="mla_attn",
    )(q, k, vt)


def _conv_kernel(x_ref, g_ref, win_ref, cw_ref, cb_ref, o_ref, tail_scr, *, tm, dm):
    @pl.when(pl.program_id(1) == 0)
    def _():
        tail_scr[...] = jnp.zeros(tail_scr.shape, F32)

    h = _rms(x_ref[0], g_ref[...]).astype(BF16)
    bcu = _dot(h, win_ref[...])
    gb = bcu[:, :dm]
    u = bcu[:, dm:2 * dm] * bcu[:, 2 * dm:]
    tail = tail_scr[...]
    row = lax.broadcasted_iota(jnp.int32, (tm, 1), 0)
    prev1 = jnp.where(row == 0, tail[7:8], pltpu.roll(u, 1, 0))
    prev2 = jnp.where(row == 0, tail[6:7], jnp.where(row == 1, tail[7:8], pltpu.roll(u, 2, 0)))
    y = cw_ref[0:1] * prev2 + cw_ref[1:2] * prev1 + cw_ref[2:3] * u + cb_ref[...]
    o_ref[0] = (gb * y).astype(BF16)
    tail_scr[...] = u[tm - 8:, :]


def _conv_pre(x, g, win, cw, cb):
    b, s, dm = x.shape
    tm = TM_PROJ
    tok = pl.BlockSpec((1, tm, dm), lambda bi, si: (bi, si, 0))
    return pl.pallas_call(
        functools.partial(_conv_kernel, tm=tm, dm=dm),
        grid=(b, s // tm),
        in_specs=[tok, _resident(g.shape), _resident(win.shape), _resident(cw.shape),
                  _resident(cb.shape)],
        out_specs=tok,
        out_shape=jax.ShapeDtypeStruct((b, s, dm), BF16),
        scratch_shapes=[pltpu.VMEM((8, dm), F32)],
        compiler_params=_params("arbitrary", "arbitrary"),
        name="conv_pre",
    )(x, g, win, cw, cb)


def _sb_proj_kernel(x_ref, g_ref, wqk_ref, wvt_ref, qk_ref, vt_ref, *, dm, scale):
    h = _rms(x_ref[0], g_ref[...]).astype(BF16)
    r = _dot(h, wqk_ref[...])
    qk_ref[0, :, :dm] = (r[:, :dm] * scale).astype(BF16)
    qk_ref[0, :, dm:] = r[:, dm:].astype(BF16)
    vt_ref[0] = _dot_nt(wvt_ref[...], h).astype(BF16)


def _sb_proj(x, g, wqk, wvt, scale):
    b, s, dm = x.shape
    tm = TM_PROJ
    return pl.pallas_call(
        functools.partial(_sb_proj_kernel, dm=dm, scale=scale),
        grid=(b, s // tm),
        in_specs=[pl.BlockSpec((1, tm, dm), lambda bi, si: (bi, si, 0)), _resident(g.shape),
                  _resident(wqk.shape), _resident(wvt.shape)],
        out_specs=[pl.BlockSpec((1, tm, 2 * dm), lambda bi, si: (bi, si, 0)),
                   pl.BlockSpec((1, dm, tm), lambda bi, si: (bi, 0, si))],
        out_shape=[jax.ShapeDtypeStruct((b, s, 2 * dm), BF16), jax.ShapeDtypeStruct((b, dm, s), BF16)],
        compiler_params=_params("arbitrary", "arbitrary"),
        name="sb_proj",
    )(x, g, wqk, wvt)


def _sb_attn_kernel(q_ref, k_ref, vt_ref, ntri_ref, o_ref, run_scr, acc_scr, za_scr, zb_scr, *, tq, tk):
    assert tq == 2 * tk
    qi = pl.program_id(2)
    last = 2 * qi + 1
    hd_w = LANES // 2
    low = lax.broadcasted_iota(jnp.int32, (1, LANES), 1) < hd_w
    q = q_ref[0]
    zero_q = jnp.zeros_like(q)
    q_heads = [jnp.where(low, q, zero_q), jnp.where(low, zero_q, q)]
    ntri = ntri_ref[...]
    run_scr[...] = jnp.zeros(run_scr.shape, F32)
    acc_scr[...] = jnp.zeros(acc_scr.shape, F32)

    def scores(u, z_scr):
        ks = pl.multiple_of(jnp.maximum(last - u, 0) * tk, tk)
        kb = k_ref[0, pl.ds(ks, tk), :]
        for hd in range(2):
            z_scr[hd] = _dot_nt(kb, q_heads[hd])

    def finish(u, z_scr, diag_offset=None):
        ks = pl.multiple_of((last - u) * tk, tk)
        vtb = vt_ref[0, :, pl.ds(ks, tk)]
        for hd in range(2):
            rows = slice(hd * hd_w, (hd + 1) * hd_w)
            z = z_scr[hd]
            neg_abs = lax.bitcast_convert_type(
                lax.bitcast_convert_type(z, jnp.uint32) | jnp.uint32(0x80000000), F32)
            softplus = jnp.maximum(z, 0.0) + jnp.log(1.0 + jnp.exp2(neg_abs)) * LOG2_E
            mass = softplus
            if diag_offset is not None:
                keep = (lax.broadcasted_iota(jnp.int32, z.shape, 0) + diag_offset
                        < lax.broadcasted_iota(jnp.int32, z.shape, 1))
                mass = jnp.where(keep, softplus, 0.0)
            hi = mass.astype(BF16)
            lo = (mass - hi.astype(F32)).astype(BF16)
            run = run_scr[hd]
            suffix = _dot(ntri, hi) + _dot(ntri, lo) + run
            a = jnp.exp2((z - softplus) + suffix)
            if diag_offset is not None:
                a = jnp.where(keep, a, 0.0)
            run_scr[hd] = run - jnp.sum(mass, axis=0, keepdims=True)
            acc_scr[rows, :] += _dot(vtb[rows, :], a.astype(BF16))

    scores(0, za_scr)
    scores(1, zb_scr)
    finish(0, za_scr, tk)
    scores(2, za_scr)
    finish(1, zb_scr, 0)

    def body(t, carry):
        u = 2 + 2 * t
        scores(u + 1, zb_scr)
        finish(u, za_scr)
        scores(u + 2, za_scr)
        finish(u + 1, zb_scr)
        return carry

    lax.fori_loop(0, qi, body, 0)
    o_ref[0] = jnp.transpose(acc_scr[...]).astype(BF16)


def _sb_attn(qk, vt, ntri):
    b, s, w2 = qk.shape
    tq, tk = TQ_SB, TK_SB
    pairs = SB_HEADS // 2
    return pl.pallas_call(
        functools.partial(_sb_attn_kernel, tq=tq, tk=tk),
        grid=(b, pairs, s // tq),
        in_specs=[pl.BlockSpec((1, tq, LANES), lambda bi, pi, qi: (bi, qi, pi)),
                  pl.BlockSpec((1, s, LANES), lambda bi, pi, qi: (bi, 0, pairs + pi)),
                  pl.BlockSpec((1, LANES, s), lambda bi, pi, qi: (bi, pi, 0)),
                  _resident(ntri.shape)],
        out_specs=pl.BlockSpec((1, tq, LANES), lambda bi, pi, qi: (bi, qi, pi)),
        out_shape=jax.ShapeDtypeStruct((b, s, w2 // 2), BF16),
        scratch_shapes=[pltpu.VMEM((2, 1, tq), F32), pltpu.VMEM((LANES, tq), F32),
                        pltpu.VMEM((2, tk, tq), F32), pltpu.VMEM((2, tk, tq), F32)],
        compiler_params=_params("arbitrary", "arbitrary", "arbitrary"),
        name="sb_attn",
    )(qk, qk, vt, ntri)


def _mlp_kernel(x_ref, a_ref, wa_ref, g_ref, wup_ref, wdn_ref, gf_ref, o_ref, *, d_ff, final):
    x1 = x_ref[...] + _dot(a_ref[...], wa_ref[...])
    h = _rms(x1, g_ref[...]).astype(BF16)
    acc = x1
    for c in range(d_ff // FF_CHUNK):
        sl = slice(c * FF_CHUNK, (c + 1) * FF_CHUNK)
        r = jnp.maximum(_dot(h, wup_ref[:, sl]), 0.0)
        acc = acc + _dot((r * r).astype(BF16), wdn_ref[sl, :])
    if final:
        acc = _rms(acc, gf_ref[...])
    o_ref[...] = acc


def _mlp(x2, a2, wa, g, wup, wdn, gf, final):
    t, dm = x2.shape
    tm = TM_PROJ
    tok = pl.BlockSpec((tm, dm), lambda i: (i, 0))
    return pl.pallas_call(
        functools.partial(_mlp_kernel, d_ff=wup.shape[1], final=final),
        grid=(t // tm,),
        in_specs=[tok, tok, _resident(wa.shape), _resident(g.shape), _resident(wup.shape),
                  _resident(wdn.shape), _resident(gf.shape)],
        out_specs=tok,
        out_shape=jax.ShapeDtypeStruct((t, dm), F32),
        compiler_params=_params("arbitrary"),
        name="out_proj_mlp",
    )(x2, a2, wa, g, wup, wdn, gf)


def _row(v):
    return v.reshape(1, -1).astype(F32)


def _pad_heads(w, heads, width):
    k = w.shape[0]
    w = w.reshape(k, heads, width)
    return jnp.pad(w, ((0, 0), (0, 0), (0, HEAD_SLAB - width))).reshape(k, heads * HEAD_SLAB)


def _mla_layer(x, pos, invf, norm_mix, w_dq, norm_q, w_uq, w_dkv, norm_kv, w_uk, w_uv):
    w_kr = jnp.pad(w_dkv[:, MLA_KV_RANK:], ((0, 0), (MLA_NOPE, HEAD_SLAB - MLA_NOPE - MLA_ROPE)))
    wd = jnp.concatenate([w_dq, w_dkv[:, :MLA_KV_RANK], w_kr], axis=1).astype(BF16)
    wuq = _pad_heads(w_uq, MLA_HEADS, MLA_NOPE + MLA_ROPE).astype(BF16)
    wuk = _pad_heads(w_uk, MLA_HEADS, MLA_NOPE).astype(BF16)
    q, k, vt = _mla_proj(x, pos, _row(norm_mix), wd, _row(norm_q), wuq, _row(norm_kv), wuk,
                         w_uv.T.astype(BF16), invf)
    return _mla_attn(q, k, vt)


def kernel(x, positions, l0_norm_mix, l0_w_dq, l0_norm_q, l0_w_uq, l0_w_dkv, l0_norm_kv, l0_w_uk, l0_w_uv, l0_w_o, l0_norm_mlp, l0_w_up, l0_w_down, l1_norm_mix, l1_w_in, l1_conv_w, l1_conv_b, l1_w_out, l1_norm_mlp, l1_w_up, l1_w_down, l2_norm_mix, l2_w_qkv, l2_w_o, l2_norm_mlp, l2_w_up, l2_w_down, l3_norm_mix, l3_w_dq, l3_norm_q, l3_w_uq, l3_w_dkv, l3_norm_kv, l3_w_uk, l3_w_uv, l3_w_o, l3_norm_mlp, l3_w_up, l3_w_down, final_norm):
    b, s, dm = x.shape
    t = b * s
    pos = positions.reshape(b, s, 1)
    inv_freq = ROPE_THETA ** (-jnp.arange(0, MLA_ROPE, 2, dtype=F32) / MLA_ROPE)
    invf = jnp.zeros((1, LANES), F32)
    invf = invf.at[0, MLA_NOPE:MLA_NOPE + MLA_ROPE].set(jnp.concatenate([inv_freq, inv_freq]))
    idx = jnp.arange(TK_SB)
    ntri = -(idx[None, :] > idx[:, None]).astype(BF16)
    gf = _row(final_norm)

    def mlp(xc, a, wa, g, wup, wdn, final=False):
        out = _mlp(xc.reshape(t, dm), a.reshape(t, dm), wa.astype(BF16), _row(g), wup.astype(BF16),
                   wdn.astype(BF16), gf, final)
        return out.reshape(b, s, dm)

    o = _mla_layer(x, pos, invf, l0_norm_mix, l0_w_dq, l0_norm_q, l0_w_uq, l0_w_dkv, l0_norm_kv,
                   l0_w_uk, l0_w_uv)
    x = mlp(x, o, l0_w_o, l0_norm_mlp, l0_w_up, l0_w_down)

    gy = _conv_pre(x, _row(l1_norm_mix), l1_w_in.astype(BF16), l1_conv_w.astype(F32), _row(l1_conv_b))
    x = mlp(x, gy, l1_w_out, l1_norm_mlp, l1_w_up, l1_w_down)

    qk, vt = _sb_proj(x, _row(l2_norm_mix), l2_w_qkv[:, :2 * dm].astype(BF16),
                      l2_w_qkv[:, 2 * dm:].T.astype(BF16), LOG2_E / math.sqrt(dm // SB_HEADS))
    o = _sb_attn(qk, vt, ntri)
    x = mlp(x, o, l2_w_o, l2_norm_mlp, l2_w_up, l2_w_down)

    o = _mla_layer(x, pos, invf, l3_norm_mix, l3_w_dq, l3_norm_q, l3_w_uq, l3_w_dkv, l3_norm_kv,
                   l3_w_uk, l3_w_uv)
    return mlp(x, o, l3_w_o, l3_norm_mlp, l3_w_up, l3_w_down, final=True)
```

```python
import functools
import math

import jax
import jax.numpy as jnp
from jax import lax
from jax.experimental import pallas as pl
from jax.experimental.pallas import tpu as pltpu

F32 = jnp.float32
BF16 = jnp.bfloat16

EPS = 1e-6
ROPE_THETA = 10000.0
MLA_HEADS = 16
MLA_NOPE = 64
MLA_ROPE = 32
MLA_V = 64
MLA_Q_RANK = 384
MLA_KV_RANK = 256
SB_HEADS = 16

LANES = 128
BF16_ROWS = 16
HEAD_SLAB = LANES
VMEM_LIMIT = 56 * 1024 * 1024

TM_PROJ = 512
TQ_MLA = 512
TQ_SB = 512
TK_SB = 256
CW_SB = 512
LOG2_E = math.log2(math.e)
FF_CHUNK = 1024


def _params(*sem):
    return pltpu.CompilerParams(dimension_semantics=sem, vmem_limit_bytes=VMEM_LIMIT)


def _resident(shape):
    nd = len(shape)
    return pl.BlockSpec(shape, lambda *_: (0,) * nd, pipeline_mode=pl.Buffered(1))


def _rms(xf, g):
    ms = jnp.mean(xf * xf, axis=-1, keepdims=True)
    return xf * lax.rsqrt(ms + EPS) * g


def _dot(a, b):
    return jnp.dot(a, b, preferred_element_type=F32)


def _dot_nt(a, b):
    return lax.dot_general(a, b, (((1,), (1,)), ((), ())), preferred_element_type=F32)


def _mla_proj_kernel(x_ref, pos_ref, g_ref, wd_ref, nq_ref, wuq_ref, nkv_ref, wuk_ref, wuvt_ref,
                     invf_ref, q_ref, k_ref, vt_ref, *, scale):
    h = _rms(x_ref[0], g_ref[...]).astype(BF16)
    d = _dot(h, wd_ref[...])
    cq = _rms(d[:, :MLA_Q_RANK], nq_ref[...]).astype(BF16)
    ckv = _rms(d[:, MLA_Q_RANK:MLA_Q_RANK + MLA_KV_RANK], nkv_ref[...]).astype(BF16)
    kr = d[:, MLA_Q_RANK + MLA_KV_RANK:]

    ang = pos_ref[0].astype(F32) * invf_ref[...]
    lane = lax.broadcasted_iota(jnp.int32, (1, LANES), 1)
    first_half = lane < MLA_NOPE + MLA_ROPE // 2
    cos = jnp.cos(ang)
    sin = jnp.sin(ang)
    sin = jnp.where(first_half, -sin, sin)

    def rotate(t, c, s):
        half = MLA_ROPE // 2
        partner = jnp.where(first_half, pltpu.roll(t, LANES - half, 1), pltpu.roll(t, half, 1))
        return t * c + partner * s

    kr = rotate(kr, cos, sin)
    cq_s = cos * scale
    sq_s = sin * scale
    qf = _dot(cq, wuq_ref[...])
    kf = _dot(ckv, wuk_ref[...])
    for hd in range(MLA_HEADS):
        sl = slice(hd * HEAD_SLAB, (hd + 1) * HEAD_SLAB)
        q_ref[0, :, sl] = rotate(qf[:, sl], cq_s, sq_s).astype(BF16)
        k_ref[0, :, sl] = (kf[:, sl] + kr).astype(BF16)
    vt_ref[0] = _dot_nt(wuvt_ref[...], ckv).astype(BF16)


def _mla_proj(x, pos, g, wd, nq, wuq, nkv, wuk, wuvt, invf):
    b, s, dm = x.shape
    tm = TM_PROJ
    hw = MLA_HEADS * HEAD_SLAB
    vw = MLA_HEADS * MLA_V
    tok = lambda w: pl.BlockSpec((1, tm, w), lambda bi, si: (bi, si, 0))
    return pl.pallas_call(
        functools.partial(_mla_proj_kernel, scale=math.log2(math.e) / math.sqrt(MLA_NOPE + MLA_ROPE)),
        grid=(b, s // tm),
        in_specs=[tok(dm), tok(1), _resident(g.shape), _resident(wd.shape), _resident(nq.shape),
                  _resident(wuq.shape), _resident(nkv.shape), _resident(wuk.shape),
                  _resident(wuvt.shape), _resident(invf.shape)],
        out_specs=[tok(hw), tok(hw), pl.BlockSpec((1, vw, tm), lambda bi, si: (bi, 0, si))],
        out_shape=[jax.ShapeDtypeStruct((b, s, hw), BF16), jax.ShapeDtypeStruct((b, s, hw), BF16),
                   jax.ShapeDtypeStruct((b, vw, s), BF16)],
        compiler_params=_params("arbitrary", "arbitrary"),
        name="mla_proj",
    )(x, pos, g, wd, nq, wuq, nkv, wuk, wuvt, invf)


def _mla_attn_kernel(q_ref, k_ref, vt_ref, o_ref, m_scr, l_scr, acc_scr, sa_scr, sb_scr, *,
                     tq):
    qi = pl.program_id(2)
    q = q_ref[0]
    m_scr[...] = jnp.full(m_scr.shape, -jnp.inf, F32)
    l_scr[...] = jnp.zeros(l_scr.shape, F32)
    acc_scr[...] = jnp.zeros(acc_scr.shape, F32)

    def scores(j, s_scr):
        ks = pl.multiple_of(j * tq, tq)
        kb = k_ref[0, pl.ds(ks, tq), :]
        for hd in range(2):
            sl = slice(hd * HEAD_SLAB, (hd + 1) * HEAD_SLAB)
            s_scr[hd] = _dot_nt(kb[:, sl], q[:, sl])

    def finish(j, s_scr, diagonal):
        ks = pl.multiple_of(j * tq, tq)
        vtb = vt_ref[0, :, pl.ds(ks, tq)]
        chunks = [(0, tq // 2, tq // 2), (tq // 2, tq // 2, tq)] if diagonal else [(0, tq, tq)]
        ones = jnp.ones((BF16_ROWS, tq), BF16)
        for hd in range(2):
            rows = slice(hd * MLA_V, (hd + 1) * MLA_V)
            vt_ext = jnp.concatenate([vtb[rows, :], ones], axis=0)
            for q0, nq, nk in chunks:
                cols = slice(q0, q0 + nq)
                st = s_scr[hd, :nk, cols]
                if diagonal:
                    key = lax.broadcasted_iota(jnp.int32, st.shape, 0)
                    qry = lax.broadcasted_iota(jnp.int32, st.shape, 1) + q0
                    st = jnp.where(key <= qry, st, -jnp.inf)
                m_old = m_scr[hd, :, cols]
                m_new = jnp.maximum(m_old, jnp.max(st, axis=0, keepdims=True))
                alpha = jnp.exp2(m_old - m_new)
                pv = _dot(vt_ext[:, :nk], jnp.exp2(st - m_new).astype(BF16))
                l_scr[hd, :, cols] = alpha * l_scr[hd, :, cols] + pv[MLA_V:MLA_V + 1]
                m_scr[hd, :, cols] = m_new
                acc_scr[rows, cols] = acc_scr[rows, cols] * alpha + pv[:MLA_V]

    scores(0, sa_scr)

    def body(t, carry):
        scores(2 * t + 1, sb_scr)
        finish(2 * t, sa_scr, False)
        scores(2 * t + 2, sa_scr)
        finish(2 * t + 1, sb_scr, False)
        return carry

    lax.fori_loop(0, qi // 2, body, 0)

    @pl.when(qi % 2 == 0)
    def _():
        finish(qi, sa_scr, True)

    @pl.when(qi % 2 == 1)
    def _():
        scores(qi, sb_scr)
        finish(qi - 1, sa_scr, False)
        finish(qi, sb_scr, True)

    for hd in range(2):
        rows = slice(hd * MLA_V, (hd + 1) * MLA_V)
        acc_scr[rows, :] = acc_scr[rows, :] * (1.0 / l_scr[hd])
    o_ref[0] = jnp.transpose(acc_scr[...]).astype(BF16)


def _mla_attn(q, k, vt):
    b, s, _ = q.shape
    tq = TQ_MLA
    pairs = MLA_HEADS // 2
    return pl.pallas_call(
        functools.partial(_mla_attn_kernel, tq=tq),
        grid=(b, pairs, s // tq),
        in_specs=[pl.BlockSpec((1, tq, 2 * HEAD_SLAB), lambda bi, pi, qi: (bi, qi, pi)),
                  pl.BlockSpec((1, s, 2 * HEAD_SLAB), lambda bi, pi, qi: (bi, 0, pi)),
                  pl.BlockSpec((1, 2 * MLA_V, s), lambda bi, pi, qi: (bi, pi, 0))],
        out_specs=pl.BlockSpec((1, tq, 2 * MLA_V), lambda bi, pi, qi: (bi, qi, pi)),
        out_shape=jax.ShapeDtypeStruct((b, s, MLA_HEADS * MLA_V), BF16),
        scratch_shapes=[pltpu.VMEM((2, 1, tq), F32), pltpu.VMEM((2, 1, tq), F32),
                        pltpu.VMEM((2 * MLA_V, tq), F32),
                        pltpu.VMEM((2, tq, tq), F32), pltpu.VMEM((2, tq, tq), F32)],
        compiler_params=_params("arbitrary", "arbitrary", "arbitrary"),
        name="mla_attn",
    )(q, k, vt)


def _conv_kernel(x_ref, g_ref, win_ref, cw_ref, cb_ref, o_ref, tail_scr, *, tm, dm):
    @pl.when(pl.program_id(1) == 0)
    def _():
        tail_scr[...] = jnp.zeros(tail_scr.shape, F32)

    h = _rms(x_ref[0], g_ref[...]).astype(BF16)
    bcu = _dot(h, win_ref[...])
    gb = bcu[:, :dm]
    u = bcu[:, dm:2 * dm] * bcu[:, 2 * dm:]
    tail = tail_scr[...]
    row = lax.broadcasted_iota(jnp.int32, (tm, 1), 0)
    prev1 = jnp.where(row == 0, tail[7:8], pltpu.roll(u, 1, 0))
    prev2 = jnp.where(row == 0, tail[6:7], jnp.where(row == 1, tail[7:8], pltpu.roll(u, 2, 0)))
    y = cw_ref[0:1] * prev2 + cw_ref[1:2] * prev1 + cw_ref[2:3] * u + cb_ref[...]
    o_ref[0] = (gb * y).astype(BF16)
    tail_scr[...] = u[tm - 8:, :]


def _conv_pre(x, g, win, cw, cb):
    b, s, dm = x.shape
    tm = TM_PROJ
    tok = pl.BlockSpec((1, tm, dm), lambda bi, si: (bi, si, 0))
    return pl.pallas_call(
        functools.partial(_conv_kernel, tm=tm, dm=dm),
        grid=(b, s // tm),
        in_specs=[tok, _resident(g.shape), _resident(win.shape), _resident(cw.shape),
                  _resident(cb.shape)],
        out_specs=tok,
        out_shape=jax.ShapeDtypeStruct((b, s, dm), BF16),
        scratch_shapes=[pltpu.VMEM((8, dm), F32)],
        compiler_params=_params("arbitrary", "arbitrary"),
        name="conv_pre",
    )(x, g, win, cw, cb)


def _sb_proj_kernel(x_ref, g_ref, wqk_ref, wvt_ref, qk_ref, vt_ref, *, dm, scale):
    h = _rms(x_ref[0], g_ref[...]).astype(BF16)
    r = _dot(h, wqk_ref[...])
    qk_ref[0, :, :dm] = (r[:, :dm] * scale).astype(BF16)
    qk_ref[0, :, dm:] = r[:, dm:].astype(BF16)
    vt_ref[0] = _dot_nt(wvt_ref[...], h).astype(BF16)


def _sb_proj(x, g, wqk, wvt, scale):
    b, s, dm = x.shape
    tm = TM_PROJ
    return pl.pallas_call(
        functools.partial(_sb_proj_kernel, dm=dm, scale=scale),
        grid=(b, s // tm),
        in_specs=[pl.BlockSpec((1, tm, dm), lambda bi, si: (bi, si, 0)), _resident(g.shape),
                  _resident(wqk.shape), _resident(wvt.shape)],
        out_specs=[pl.BlockSpec((1, tm, 2 * dm), lambda bi, si: (bi, si, 0)),
                   pl.BlockSpec((1, dm, tm), lambda bi, si: (bi, 0, si))],
        out_shape=[jax.ShapeDtypeStruct((b, s, 2 * dm), BF16), jax.ShapeDtypeStruct((b, dm, s), BF16)],
        compiler_params=_params("arbitrary", "arbitrary"),
        name="sb_proj",
    )(x, g, wqk, wvt)


def _sb_attn_kernel(q_ref, k_ref, vt_ref, ntri_ref, o_ref, run_scr, acc_scr, za_scr, zb_scr, *, tq, tk,
                    cw):
    assert tq == 2 * tk
    qi = pl.program_id(2)
    last = 2 * qi + 1
    hd_w = LANES // 2
    low = lax.broadcasted_iota(jnp.int32, (1, LANES), 1) < hd_w
    q = q_ref[0]
    zero_q = jnp.zeros_like(q)
    q_heads = [jnp.where(low, q, zero_q), jnp.where(low, zero_q, q)]
    ntri = ntri_ref[...]
    run_scr[...] = jnp.zeros(run_scr.shape, F32)
    acc_scr[...] = jnp.zeros(acc_scr.shape, F32)

    def scores(u, z_scr):
        ks = pl.multiple_of(jnp.maximum(last - u, 0) * tk, tk)
        kb = k_ref[0, pl.ds(ks, tk), :]
        for hd in range(2):
            z_scr[hd] = _dot_nt(kb, q_heads[hd])

    def finish(u, z_scr, diag_offset=None):
        ks = pl.multiple_of((last - u) * tk, tk)
        vtb = vt_ref[0, :, pl.ds(ks, tk)]

        def masses(hd, c):
            z = z_scr[hd, :, c * cw:(c + 1) * cw]
            neg_abs = lax.bitcast_convert_type(
                lax.bitcast_convert_type(z, jnp.uint32) | jnp.uint32(0x80000000), F32)
            softplus = jnp.maximum(z, 0.0) + jnp.log(1.0 + jnp.exp2(neg_abs)) * LOG2_E
            keep = None
            mass = softplus
            if diag_offset is not None:
                keep = (lax.broadcasted_iota(jnp.int32, z.shape, 0) + diag_offset
                        < lax.broadcasted_iota(jnp.int32, z.shape, 1) + c * cw)
                mass = jnp.where(keep, softplus, 0.0)
            hi = mass.astype(BF16)
            lo = (mass - hi.astype(F32)).astype(BF16)
            sums = _dot(ntri, hi) + _dot(ntri, lo)
            return z - softplus, keep, sums

        def weights(hd, c, log_beta, keep, sums):
            rows = slice(hd * hd_w, (hd + 1) * hd_w)
            cols = slice(c * cw, (c + 1) * cw)
            a = jnp.exp2(log_beta + sums[:tk])
            if keep is not None:
                a = jnp.where(keep, a, 0.0)
            run = run_scr[hd, :, cols]
            acc_scr[rows, cols] += _dot(vtb[rows, :], a.astype(BF16)) * jnp.exp2(run)
            run_scr[hd, :, cols] = run + sums[tk:tk + 1]

        chains = [(hd, c) for hd in range(2) for c in range(tq // cw)]
        staged = None
        for ch in chains:
            nxt = masses(*ch)
            if staged is not None:
                weights(*staged)
            staged = ch + nxt
        weights(*staged)

    scores(0, za_scr)
    scores(1, zb_scr)
    finish(0, za_scr, tk)
    scores(2, za_scr)
    finish(1, zb_scr, 0)

    def body(t, carry):
        u = 2 + 2 * t
        scores(u + 1, zb_scr)
        finish(u, za_scr)
        scores(u + 2, za_scr)
        finish(u + 1, zb_scr)
        return carry

    lax.fori_loop(0, qi, body, 0)
    o_ref[0] = jnp.transpose(acc_scr[...]).astype(BF16)


def _sb_attn(qk, vt, ntri):
    b, s, w2 = qk.shape
    tq, tk = TQ_SB, TK_SB
    pairs = SB_HEADS // 2
    return pl.pallas_call(
        functools.partial(_sb_attn_kernel, tq=tq, tk=tk, cw=CW_SB),
        grid=(b, pairs, s // tq),
        in_specs=[pl.BlockSpec((1, tq, LANES), lambda bi, pi, qi: (bi, qi, pi)),
                  pl.BlockSpec((1, s, LANES), lambda bi, pi, qi: (bi, 0, pairs + pi)),
                  pl.BlockSpec((1, LANES, s), lambda bi, pi, qi: (bi, pi, 0)),
                  _resident(ntri.shape)],
        out_specs=pl.BlockSpec((1, tq, LANES), lambda bi, pi, qi: (bi, qi, pi)),
        out_shape=jax.ShapeDtypeStruct((b, s, w2 // 2), BF16),
        scratch_shapes=[pltpu.VMEM((2, 1, tq), F32), pltpu.VMEM((LANES, tq), F32),
                        pltpu.VMEM((2, tk, tq), F32), pltpu.VMEM((2, tk, tq), F32)],
        compiler_params=_params("arbitrary", "arbitrary", "arbitrary"),
        name="sb_attn",
    )(qk, qk, vt, ntri)


def _mlp_kernel(x_ref, a_ref, wa_ref, g_ref, wup_ref, wdn_ref, gf_ref, o_ref, *, d_ff, final):
    x1 = x_ref[...] + _dot(a_ref[...], wa_ref[...])
    h = _rms(x1, g_ref[...]).astype(BF16)
    acc = x1
    for c in range(d_ff // FF_CHUNK):
        sl = slice(c * FF_CHUNK, (c + 1) * FF_CHUNK)
        r = jnp.maximum(_dot(h, wup_ref[:, sl]), 0.0)
        acc = acc + _dot((r * r).astype(BF16), wdn_ref[sl, :])
    if final:
        acc = _rms(acc, gf_ref[...])
    o_ref[...] = acc


def _mlp(x2, a2, wa, g, wup, wdn, gf, final):
    t, dm = x2.shape
    tm = TM_PROJ
    tok = pl.BlockSpec((tm, dm), lambda i: (i, 0))
    return pl.pallas_call(
        functools.partial(_mlp_kernel, d_ff=wup.shape[1], final=final),
        grid=(t // tm,),
        in_specs=[tok, tok, _resident(wa.shape), _resident(g.shape), _resident(wup.shape),
                  _resident(wdn.shape), _resident(gf.shape)],
        out_specs=tok,
        out_shape=jax.ShapeDtypeStruct((t, dm), F32),
        compiler_params=_params("arbitrary"),
        name="out_proj_mlp",
    )(x2, a2, wa, g, wup, wdn, gf)


def _row(v):
    return v.reshape(1, -1).astype(F32)


def _pad_heads(w, heads, width):
    k = w.shape[0]
    w = w.reshape(k, heads, width)
    return jnp.pad(w, ((0, 0), (0, 0), (0, HEAD_SLAB - width))).reshape(k, heads * HEAD_SLAB)


def _mla_layer(x, pos, invf, norm_mix, w_dq, norm_q, w_uq, w_dkv, norm_kv, w_uk, w_uv):
    w_kr = jnp.pad(w_dkv[:, MLA_KV_RANK:], ((0, 0), (MLA_NOPE, HEAD_SLAB - MLA_NOPE - MLA_ROPE)))
    wd = jnp.concatenate([w_dq, w_dkv[:, :MLA_KV_RANK], w_kr], axis=1).astype(BF16)
    wuq = _pad_heads(w_uq, MLA_HEADS, MLA_NOPE + MLA_ROPE).astype(BF16)
    wuk = _pad_heads(w_uk, MLA_HEADS, MLA_NOPE).astype(BF16)
    q, k, vt = _mla_proj(x, pos, _row(norm_mix), wd, _row(norm_q), wuq, _row(norm_kv), wuk,
                         w_uv.T.astype(BF16), invf)
    return _mla_attn(q, k, vt)


def kernel(x, positions, l0_norm_mix, l0_w_dq, l0_norm_q, l0_w_uq, l0_w_dkv, l0_norm_kv, l0_w_uk, l0_w_uv, l0_w_o, l0_norm_mlp, l0_w_up, l0_w_down, l1_norm_mix, l1_w_in, l1_conv_w, l1_conv_b, l1_w_out, l1_norm_mlp, l1_w_up, l1_w_down, l2_norm_mix, l2_w_qkv, l2_w_o, l2_norm_mlp, l2_w_up, l2_w_down, l3_norm_mix, l3_w_dq, l3_norm_q, l3_w_uq, l3_w_dkv, l3_norm_kv, l3_w_uk, l3_w_uv, l3_w_o, l3_norm_mlp, l3_w_up, l3_w_down, final_norm):
    b, s, dm = x.shape
    t = b * s
    pos = positions.reshape(b, s, 1)
    inv_freq = ROPE_THETA ** (-jnp.arange(0, MLA_ROPE, 2, dtype=F32) / MLA_ROPE)
    invf = jnp.zeros((1, LANES), F32)
    invf = invf.at[0, MLA_NOPE:MLA_NOPE + MLA_ROPE].set(jnp.concatenate([inv_freq, inv_freq]))
    tri_row = jnp.arange(TK_SB + BF16_ROWS)[:, None]
    ntri = -((jnp.arange(TK_SB)[None, :] > tri_row) | (tri_row >= TK_SB)).astype(BF16)
    gf = _row(final_norm)

    def mlp(xc, a, wa, g, wup, wdn, final=False):
        out = _mlp(xc.reshape(t, dm), a.reshape(t, dm), wa.astype(BF16), _row(g), wup.astype(BF16),
                   wdn.astype(BF16), gf, final)
        return out.reshape(b, s, dm)

    o = _mla_layer(x, pos, invf, l0_norm_mix, l0_w_dq, l0_norm_q, l0_w_uq, l0_w_dkv, l0_norm_kv,
                   l0_w_uk, l0_w_uv)
    x = mlp(x, o, l0_w_o, l0_norm_mlp, l0_w_up, l0_w_down)

    gy = _conv_pre(x, _row(l1_norm_mix), l1_w_in.astype(BF16), l1_conv_w.astype(F32), _row(l1_conv_b))
    x = mlp(x, gy, l1_w_out, l1_norm_mlp, l1_w_up, l1_w_down)

    qk, vt = _sb_proj(x, _row(l2_norm_mix), l2_w_qkv[:, :2 * dm].astype(BF16),
                      l2_w_qkv[:, 2 * dm:].T.astype(BF16), LOG2_E / math.sqrt(dm // SB_HEADS))
    o = _sb_attn(qk, vt, ntri)
    x = mlp(x, o, l2_w_o, l2_norm_mlp, l2_w_up, l2_w_down)

    o = _mla_layer(x, pos, invf, l3_norm_mix, l3_w_dq, l3_norm_q, l3_w_uq, l3_w_dkv, l3_norm_kv,
                   l3_w_uk, l3_w_uv)
    return mlp(x, o, l3_w_o, l3_norm_mlp, l3_w_up, l3_w_down, final=True)
```

```python
import functools
import math

import jax
import jax.numpy as jnp
from jax import lax
from jax.experimental import pallas as pl
from jax.experimental.pallas import tpu as pltpu

F32 = jnp.float32
BF16 = jnp.bfloat16

EPS = 1e-6
ROPE_THETA = 10000.0
MLA_HEADS = 16
MLA_NOPE = 64
MLA_ROPE = 32
MLA_V = 64
MLA_Q_RANK = 384
MLA_KV_RANK = 256
SB_HEADS = 16

LANES = 128
BF16_ROWS = 16
HEAD_SLAB = LANES
VMEM_LIMIT = 56 * 1024 * 1024

TM_PROJ = 512
TQ_MLA = 512
TQ_SB = 512
TK_SB = 256
CW_SB = 512
LOG2_E = math.log2(math.e)
FF_CHUNK = 1024


def _params(*sem):
    return pltpu.CompilerParams(dimension_semantics=sem, vmem_limit_bytes=VMEM_LIMIT)


def _resident(shape):
    nd = len(shape)
    return pl.BlockSpec(shape, lambda *_: (0,) * nd, pipeline_mode=pl.Buffered(1))


def _rms(xf, g):
    ms = jnp.mean(xf * xf, axis=-1, keepdims=True)
    return xf * lax.rsqrt(ms + EPS) * g


def _dot(a, b):
    return jnp.dot(a, b, preferred_element_type=F32)


def _dot_nt(a, b):
    return lax.dot_general(a, b, (((1,), (1,)), ((), ())), preferred_element_type=F32)


def _mla_proj_kernel(x_ref, pos_ref, posr_ref, g_ref, wd_ref, nq_ref, wuqt_ref, nkv_ref, wuk_ref, wuvt_ref,
                     invf_ref, invfc_ref, qt_ref, k_ref, vt_ref, *, scale):
    h = _rms(x_ref[0], g_ref[...]).astype(BF16)
    d = _dot(h, wd_ref[...])
    cq = _rms(d[:, :MLA_Q_RANK], nq_ref[...]).astype(BF16)
    ckv = _rms(d[:, MLA_Q_RANK:MLA_Q_RANK + MLA_KV_RANK], nkv_ref[...]).astype(BF16)
    kr = d[:, MLA_Q_RANK + MLA_KV_RANK:]

    ang = pos_ref[0].astype(F32) * invf_ref[...]
    lane = lax.broadcasted_iota(jnp.int32, (1, LANES), 1)
    first_half = lane < MLA_NOPE + MLA_ROPE // 2
    cos = jnp.cos(ang)
    sin = jnp.sin(ang)
    sin = jnp.where(first_half, -sin, sin)

    def rotate(t, c, s):
        half = MLA_ROPE // 2
        partner = jnp.where(first_half, pltpu.roll(t, LANES - half, 1), pltpu.roll(t, half, 1))
        return t * c + partner * s

    kr = rotate(kr, cos, sin)
    kf = _dot(ckv, wuk_ref[...])
    for hd in range(MLA_HEADS):
        sl = slice(hd * HEAD_SLAB, (hd + 1) * HEAD_SLAB)
        k_ref[0, :, sl] = (kf[:, sl] + kr).astype(BF16)
    vt_ref[0] = _dot_nt(wuvt_ref[...], ckv).astype(BF16)

    qft = _dot_nt(wuqt_ref[...], cq)
    ang_t = invfc_ref[...] * posr_ref[0].astype(F32)
    cos_t = jnp.cos(ang_t) * scale
    sin_t = jnp.sin(ang_t) * scale
    half = MLA_ROPE // 2
    for hd in range(MLA_HEADS):
        r0 = hd * HEAD_SLAB
        r1, r2, r3 = r0 + MLA_NOPE, r0 + MLA_NOPE + half, r0 + MLA_NOPE + MLA_ROPE
        x1, x2 = qft[r1:r2], qft[r2:r3]
        qt_ref[0, r0:r1] = (qft[r0:r1] * scale).astype(BF16)
        qt_ref[0, r1:r2] = (x1 * cos_t - x2 * sin_t).astype(BF16)
        qt_ref[0, r2:r3] = (x2 * cos_t + x1 * sin_t).astype(BF16)
        qt_ref[0, r3:r0 + HEAD_SLAB] = jnp.zeros((HEAD_SLAB - MLA_NOPE - MLA_ROPE, qft.shape[1]), BF16)


def _mla_proj(x, pos, posr, g, wd, nq, wuqt, nkv, wuk, wuvt, invf, invfc):
    b, s, dm = x.shape
    tm = TM_PROJ
    hw = MLA_HEADS * HEAD_SLAB
    vw = MLA_HEADS * MLA_V
    tok = lambda w: pl.BlockSpec((1, tm, w), lambda bi, si: (bi, si, 0))
    tok_t = lambda r: pl.BlockSpec((1, r, tm), lambda bi, si: (bi, 0, si))
    return pl.pallas_call(
        functools.partial(_mla_proj_kernel, scale=LOG2_E / math.sqrt(MLA_NOPE + MLA_ROPE)),
        grid=(b, s // tm),
        in_specs=[tok(dm), tok(1), tok_t(1), _resident(g.shape), _resident(wd.shape), _resident(nq.shape),
                  _resident(wuqt.shape), _resident(nkv.shape), _resident(wuk.shape),
                  _resident(wuvt.shape), _resident(invf.shape), _resident(invfc.shape)],
        out_specs=[tok_t(hw), tok(hw), tok_t(vw)],
        out_shape=[jax.ShapeDtypeStruct((b, hw, s), BF16), jax.ShapeDtypeStruct((b, s, hw), BF16),
                   jax.ShapeDtypeStruct((b, vw, s), BF16)],
        compiler_params=_params("arbitrary", "arbitrary"),
        name="mla_proj",
    )(x, pos, posr, g, wd, nq, wuqt, nkv, wuk, wuvt, invf, invfc)


def _mla_attn_kernel(qt_ref, k_ref, vt_ref, o_ref, m_scr, l_scr, acc_scr, sa_scr, sb_scr, *,
                     tq):
    qi = pl.program_id(2)
    m_scr[...] = jnp.full(m_scr.shape, -jnp.inf, F32)
    l_scr[...] = jnp.zeros(l_scr.shape, F32)
    acc_scr[...] = jnp.zeros(acc_scr.shape, F32)

    def scores(j, s_scr):
        ks = pl.multiple_of(j * tq, tq)
        for hd in range(2):
            sl = slice(hd * HEAD_SLAB, (hd + 1) * HEAD_SLAB)
            s_scr[hd] = _dot(k_ref[0, pl.ds(ks, tq), sl], qt_ref[0, sl, :])

    def finish(j, s_scr, diagonal):
        ks = pl.multiple_of(j * tq, tq)
        vtb = vt_ref[0, :, pl.ds(ks, tq)]
        chunks = [(0, tq // 2, tq // 2), (tq // 2, tq // 2, tq)] if diagonal else [(0, tq, tq)]
        ones = jnp.ones((BF16_ROWS, tq), BF16)
        for hd in range(2):
            rows = slice(hd * MLA_V, (hd + 1) * MLA_V)
            vt_ext = jnp.concatenate([vtb[rows, :], ones], axis=0)
            for q0, nq, nk in chunks:
                cols = slice(q0, q0 + nq)
                st = s_scr[hd, :nk, cols]
                if diagonal:
                    key = lax.broadcasted_iota(jnp.int32, st.shape, 0)
                    qry = lax.broadcasted_iota(jnp.int32, st.shape, 1) + q0
                    st = jnp.where(key <= qry, st, -jnp.inf)
                m_old = m_scr[hd, :, cols]
                m_new = jnp.maximum(m_old, jnp.max(st, axis=0, keepdims=True))
                alpha = jnp.exp2(m_old - m_new)
                pv = _dot(vt_ext[:, :nk], jnp.exp2(st - m_new).astype(BF16))
                l_scr[hd, :, cols] = alpha * l_scr[hd, :, cols] + pv[MLA_V:MLA_V + 1]
                m_scr[hd, :, cols] = m_new
                acc_scr[rows, cols] = acc_scr[rows, cols] * alpha + pv[:MLA_V]

    scores(0, sa_scr)

    def body(t, carry):
        scores(2 * t + 1, sb_scr)
        finish(2 * t, sa_scr, False)
        scores(2 * t + 2, sa_scr)
        finish(2 * t + 1, sb_scr, False)
        return carry

    lax.fori_loop(0, qi // 2, body, 0)

    @pl.when(qi % 2 == 0)
    def _():
        finish(qi, sa_scr, True)

    @pl.when(qi % 2 == 1)
    def _():
        scores(qi, sb_scr)
        finish(qi - 1, sa_scr, False)
        finish(qi, sb_scr, True)

    for hd in range(2):
        rows = slice(hd * MLA_V, (hd + 1) * MLA_V)
        acc_scr[rows, :] = acc_scr[rows, :] * (1.0 / l_scr[hd])
    o_ref[0] = jnp.transpose(acc_scr[...]).astype(BF16)


def _mla_attn(qt, k, vt):
    b, s, _ = k.shape
    tq = TQ_MLA
    pairs = MLA_HEADS // 2
    return pl.pallas_call(
        functools.partial(_mla_attn_kernel, tq=tq),
        grid=(b, pairs, s // tq),
        in_specs=[pl.BlockSpec((1, 2 * HEAD_SLAB, tq), lambda bi, pi, qi: (bi, pi, qi)),
                  pl.BlockSpec((1, s, 2 * HEAD_SLAB), lambda bi, pi, qi: (bi, 0, pi)),
                  pl.BlockSpec((1, 2 * MLA_V, s), lambda bi, pi, qi: (bi, pi, 0))],
        out_specs=pl.BlockSpec((1, tq, 2 * MLA_V), lambda bi, pi, qi: (bi, qi, pi)),
        out_shape=jax.ShapeDtypeStruct((b, s, MLA_HEADS * MLA_V), BF16),
        scratch_shapes=[pltpu.VMEM((2, 1, tq), F32), pltpu.VMEM((2, 1, tq), F32),
                        pltpu.VMEM((2 * MLA_V, tq), F32),
                        pltpu.VMEM((2, tq, tq), F32), pltpu.VMEM((2, tq, tq), F32)],
        compiler_params=_params("arbitrary", "arbitrary", "arbitrary"),
        name="mla_attn",
    )(qt, k, vt)


def _conv_kernel(x_ref, g_ref, win_ref, cw_ref, cb_ref, o_ref, tail_scr, *, tm, dm):
    @pl.when(pl.program_id(1) == 0)
    def _():
        tail_scr[...] = jnp.zeros(tail_scr.shape, F32)

    h = _rms(x_ref[0], g_ref[...]).astype(BF16)
    bcu = _dot(h, win_ref[...])
    gb = bcu[:, :dm]
    u = bcu[:, dm:2 * dm] * bcu[:, 2 * dm:]
    tail = tail_scr[...]
    row = lax.broadcasted_iota(jnp.int32, (tm, 1), 0)
    prev1 = jnp.where(row == 0, tail[7:8], pltpu.roll(u, 1, 0))
    prev2 = jnp.where(row == 0, tail[6:7], jnp.where(row == 1, tail[7:8], pltpu.roll(u, 2, 0)))
    y = cw_ref[0:1] * prev2 + cw_ref[1:2] * prev1 + cw_ref[2:3] * u + cb_ref[...]
    o_ref[0] = (gb * y).astype(BF16)
    tail_scr[...] = u[tm - 8:, :]


def _conv_pre(x, g, win, cw, cb):
    b, s, dm = x.shape
    tm = TM_PROJ
    tok = pl.BlockSpec((1, tm, dm), lambda bi, si: (bi, si, 0))
    return pl.pallas_call(
        functools.partial(_conv_kernel, tm=tm, dm=dm),
        grid=(b, s // tm),
        in_specs=[tok, _resident(g.shape), _resident(win.shape), _resident(cw.shape),
                  _resident(cb.shape)],
        out_specs=tok,
        out_shape=jax.ShapeDtypeStruct((b, s, dm), BF16),
        scratch_shapes=[pltpu.VMEM((8, dm), F32)],
        compiler_params=_params("arbitrary", "arbitrary"),
        name="conv_pre",
    )(x, g, win, cw, cb)


def _sb_proj_kernel(x_ref, g_ref, wk_ref, wqvt_ref, k_ref, qvt_ref, *, dm, scale):
    h = _rms(x_ref[0], g_ref[...]).astype(BF16)
    k_ref[0] = _dot(h, wk_ref[...]).astype(BF16)
    t = _dot_nt(wqvt_ref[...], h)
    qvt_ref[0, :dm] = (t[:dm] * scale).astype(BF16)
    qvt_ref[0, dm:] = t[dm:].astype(BF16)


def _sb_proj(x, g, wk, wqvt, scale):
    b, s, dm = x.shape
    tm = TM_PROJ
    return pl.pallas_call(
        functools.partial(_sb_proj_kernel, dm=dm, scale=scale),
        grid=(b, s // tm),
        in_specs=[pl.BlockSpec((1, tm, dm), lambda bi, si: (bi, si, 0)), _resident(g.shape),
                  _resident(wk.shape), _resident(wqvt.shape)],
        out_specs=[pl.BlockSpec((1, tm, dm), lambda bi, si: (bi, si, 0)),
                   pl.BlockSpec((1, 2 * dm, tm), lambda bi, si: (bi, 0, si))],
        out_shape=[jax.ShapeDtypeStruct((b, s, dm), BF16), jax.ShapeDtypeStruct((b, 2 * dm, s), BF16)],
        compiler_params=_params("arbitrary", "arbitrary"),
        name="sb_proj",
    )(x, g, wk, wqvt)


def _sb_attn_kernel(qt_ref, k_ref, vt_ref, ntri_ref, o_ref, run_scr, acc_scr, za_scr, zb_scr, qh_scr, *,
                    tq, tk, cw):
    assert tq == 2 * tk
    qi = pl.program_id(2)
    last = 2 * qi + 1
    hd_w = LANES // 2
    low = lax.broadcasted_iota(jnp.int32, (LANES, 1), 0) < hd_w
    qt = qt_ref[0]
    zero_q = jnp.zeros_like(qt)
    qh_scr[0] = jnp.where(low, qt, zero_q)
    qh_scr[1] = jnp.where(low, zero_q, qt)
    run_scr[...] = jnp.zeros(run_scr.shape, F32)
    acc_scr[...] = jnp.zeros(acc_scr.shape, F32)

    def scores(u, z_scr):
        ks = pl.multiple_of(jnp.maximum(last - u, 0) * tk, tk)
        kb = k_ref[0, pl.ds(ks, tk), :]
        for hd in range(2):
            z_scr[hd] = _dot(kb, qh_scr[hd])

    def finish(u, z_scr, diag_offset=None):
        ks = pl.multiple_of((last - u) * tk, tk)
        vtb = vt_ref[0, :, pl.ds(ks, tk)]

        def masses(hd, c):
            z = z_scr[hd, :, c * cw:(c + 1) * cw]
            neg_abs = lax.bitcast_convert_type(
                lax.bitcast_convert_type(z, jnp.uint32) | jnp.uint32(0x80000000), F32)
            softplus = jnp.maximum(z, 0.0) + jnp.log(1.0 + jnp.exp2(neg_abs)) * LOG2_E
            keep = None
            mass = softplus
            if diag_offset is not None:
                keep = (lax.broadcasted_iota(jnp.int32, z.shape, 0) + diag_offset
                        < lax.broadcasted_iota(jnp.int32, z.shape, 1) + c * cw)
                mass = jnp.where(keep, softplus, 0.0)
            ntri = ntri_ref[...]
            sums = _dot(ntri, mass.astype(BF16))
            return z - softplus, keep, sums

        def weights(hd, c, log_beta, keep, sums):
            rows = slice(hd * hd_w, (hd + 1) * hd_w)
            cols = slice(c * cw, (c + 1) * cw)
            a = jnp.exp2(log_beta + sums[:tk])
            if keep is not None:
                a = jnp.where(keep, a, 0.0)
            run = run_scr[hd, :, cols]
            acc_scr[rows, cols] += _dot(vtb[rows, :], a.astype(BF16)) * jnp.exp2(run)
            run_scr[hd, :, cols] = run + sums[tk:tk + 1]

        chains = [(hd, c) for hd in range(2) for c in range(tq // cw)]
        staged = None
        for ch in chains:
            nxt = masses(*ch)
            if staged is not None:
                weights(*staged)
            staged = ch + nxt
        weights(*staged)

    scores(0, za_scr)
    scores(1, zb_scr)
    finish(0, za_scr, tk)
    scores(2, za_scr)
    finish(1, zb_scr, 0)

    def body(t, carry):
        u = 2 + 2 * t
        scores(u + 1, zb_scr)
        finish(u, za_scr)
        scores(u + 2, za_scr)
        finish(u + 1, zb_scr)
        return carry

    lax.fori_loop(0, qi, body, 0)
    o_ref[0] = jnp.transpose(acc_scr[...]).astype(BF16)


def _sb_attn(k, qvt, ntri):
    b, s, dm = k.shape
    tq, tk = TQ_SB, TK_SB
    pairs = SB_HEADS // 2
    return pl.pallas_call(
        functools.partial(_sb_attn_kernel, tq=tq, tk=tk, cw=CW_SB),
        grid=(b, pairs, s // tq),
        in_specs=[pl.BlockSpec((1, LANES, tq), lambda bi, pi, qi: (bi, pi, qi)),
                  pl.BlockSpec((1, s, LANES), lambda bi, pi, qi: (bi, 0, pi)),
                  pl.BlockSpec((1, LANES, s), lambda bi, pi, qi: (bi, pairs + pi, 0)),
                  _resident(ntri.shape)],
        out_specs=pl.BlockSpec((1, tq, LANES), lambda bi, pi, qi: (bi, qi, pi)),
        out_shape=jax.ShapeDtypeStruct((b, s, dm), BF16),
        scratch_shapes=[pltpu.VMEM((2, 1, tq), F32), pltpu.VMEM((LANES, tq), F32),
                        pltpu.VMEM((2, tk, tq), F32), pltpu.VMEM((2, tk, tq), F32),
                        pltpu.VMEM((2, LANES, tq), BF16)],
        compiler_params=_params("arbitrary", "arbitrary", "arbitrary"),
        name="sb_attn",
    )(qvt, k, qvt, ntri)


def _mlp_kernel(x_ref, a_ref, wa_ref, g_ref, wup_ref, wdn_ref, gf_ref, o_ref, *, d_ff, final):
    x1 = x_ref[...] + _dot(a_ref[...], wa_ref[...])
    h = _rms(x1, g_ref[...]).astype(BF16)
    acc = x1
    for c in range(d_ff // FF_CHUNK):
        sl = slice(c * FF_CHUNK, (c + 1) * FF_CHUNK)
        r = jnp.maximum(_dot(h, wup_ref[:, sl]), 0.0)
        acc = acc + _dot((r * r).astype(BF16), wdn_ref[sl, :])
    if final:
        acc = _rms(acc, gf_ref[...])
    o_ref[...] = acc


def _mlp(x2, a2, wa, g, wup, wdn, gf, final):
    t, dm = x2.shape
    tm = TM_PROJ
    tok = pl.BlockSpec((tm, dm), lambda i: (i, 0))
    return pl.pallas_call(
        functools.partial(_mlp_kernel, d_ff=wup.shape[1], final=final),
        grid=(t // tm,),
        in_specs=[tok, tok, _resident(wa.shape), _resident(g.shape), _resident(wup.shape),
                  _resident(wdn.shape), _resident(gf.shape)],
        out_specs=tok,
        out_shape=jax.ShapeDtypeStruct((t, dm), F32),
        compiler_params=_params("arbitrary"),
        name="out_proj_mlp",
    )(x2, a2, wa, g, wup, wdn, gf)


def _row(v):
    return v.reshape(1, -1).astype(F32)


def _pad_heads(w, heads, width):
    k = w.shape[0]
    w = w.reshape(k, heads, width)
    return jnp.pad(w, ((0, 0), (0, 0), (0, HEAD_SLAB - width))).reshape(k, heads * HEAD_SLAB)


def _mla_layer(x, pos, posr, invf, invfc, norm_mix, w_dq, norm_q, w_uq, w_dkv, norm_kv, w_uk, w_uv):
    w_kr = jnp.pad(w_dkv[:, MLA_KV_RANK:], ((0, 0), (MLA_NOPE, HEAD_SLAB - MLA_NOPE - MLA_ROPE)))
    wd = jnp.concatenate([w_dq, w_dkv[:, :MLA_KV_RANK], w_kr], axis=1).astype(BF16)
    wuqt = _pad_heads(w_uq, MLA_HEADS, MLA_NOPE + MLA_ROPE).T.astype(BF16)
    wuk = _pad_heads(w_uk, MLA_HEADS, MLA_NOPE).astype(BF16)
    qt, k, vt = _mla_proj(x, pos, posr, _row(norm_mix), wd, _row(norm_q), wuqt, _row(norm_kv), wuk,
                          w_uv.T.astype(BF16), invf, invfc)
    return _mla_attn(qt, k, vt)


def kernel(x, positions, l0_norm_mix, l0_w_dq, l0_norm_q, l0_w_uq, l0_w_dkv, l0_norm_kv, l0_w_uk, l0_w_uv, l0_w_o, l0_norm_mlp, l0_w_up, l0_w_down, l1_norm_mix, l1_w_in, l1_conv_w, l1_conv_b, l1_w_out, l1_norm_mlp, l1_w_up, l1_w_down, l2_norm_mix, l2_w_qkv, l2_w_o, l2_norm_mlp, l2_w_up, l2_w_down, l3_norm_mix, l3_w_dq, l3_norm_q, l3_w_uq, l3_w_dkv, l3_norm_kv, l3_w_uk, l3_w_uv, l3_w_o, l3_norm_mlp, l3_w_up, l3_w_down, final_norm):
    b, s, dm = x.shape
    t = b * s
    pos = positions.reshape(b, s, 1)
    posr = positions.reshape(b, 1, s)
    inv_freq = ROPE_THETA ** (-jnp.arange(0, MLA_ROPE, 2, dtype=F32) / MLA_ROPE)
    invfc = inv_freq.reshape(-1, 1)
    invf = jnp.zeros((1, LANES), F32)
    invf = invf.at[0, MLA_NOPE:MLA_NOPE + MLA_ROPE].set(jnp.concatenate([inv_freq, inv_freq]))
    tri_row = jnp.arange(TK_SB + BF16_ROWS)[:, None]
    ntri = -((jnp.arange(TK_SB)[None, :] > tri_row) | (tri_row >= TK_SB)).astype(BF16)
    gf = _row(final_norm)

    def mlp(xc, a, wa, g, wup, wdn, final=False):
        out = _mlp(xc.reshape(t, dm), a.reshape(t, dm), wa.astype(BF16), _row(g), wup.astype(BF16),
                   wdn.astype(BF16), gf, final)
        return out.reshape(b, s, dm)

    o = _mla_layer(x, pos, posr, invf, invfc, l0_norm_mix, l0_w_dq, l0_norm_q, l0_w_uq, l0_w_dkv, l0_norm_kv,
                   l0_w_uk, l0_w_uv)
    x = mlp(x, o, l0_w_o, l0_norm_mlp, l0_w_up, l0_w_down)

    gy = _conv_pre(x, _row(l1_norm_mix), l1_w_in.astype(BF16), l1_conv_w.astype(F32), _row(l1_conv_b))
    x = mlp(x, gy, l1_w_out, l1_norm_mlp, l1_w_up, l1_w_down)

    w_qv = jnp.concatenate([l2_w_qkv[:, :dm], l2_w_qkv[:, 2 * dm:]], axis=1)
    k, qvt = _sb_proj(x, _row(l2_norm_mix), l2_w_qkv[:, dm:2 * dm].astype(BF16), w_qv.T.astype(BF16),
                      LOG2_E / math.sqrt(dm // SB_HEADS))
    o = _sb_attn(k, qvt, ntri)
    x = mlp(x, o, l2_w_o, l2_norm_mlp, l2_w_up, l2_w_down)

    o = _mla_layer(x, pos, posr, invf, invfc, l3_norm_mix, l3_w_dq, l3_norm_q, l3_w_uq, l3_w_dkv, l3_norm_kv,
                   l3_w_uk, l3_w_uv)
    return mlp(x, o, l3_w_o, l3_norm_mlp, l3_w_up, l3_w_down, final=True)
```

```python
import functools
import math

import jax
import jax.numpy as jnp
from jax import lax
from jax.experimental import pallas as pl
from jax.experimental.pallas import tpu as pltpu

F32 = jnp.float32
BF16 = jnp.bfloat16

EPS = 1e-6
ROPE_THETA = 10000.0
MLA_HEADS = 16
MLA_NOPE = 64
MLA_ROPE = 32
MLA_V = 64
MLA_Q_RANK = 384
MLA_KV_RANK = 256
SB_HEADS = 16

LANES = 128
BF16_ROWS = 16
HEAD_SLAB = LANES
VMEM_LIMIT = 56 * 1024 * 1024

TM_PROJ = 512
TQ_MLA = 1024
TK_MLA = 512
TQ_SB = 1024
TK_SB = 256
LOG2_E = math.log2(math.e)
FF_CHUNK = 1024


def _params(*sem):
    return pltpu.CompilerParams(dimension_semantics=sem, vmem_limit_bytes=VMEM_LIMIT)


def _resident(shape):
    nd = len(shape)
    return pl.BlockSpec(shape, lambda *_: (0,) * nd, pipeline_mode=pl.Buffered(1))


def _rms(xf, g):
    ms = jnp.mean(xf * xf, axis=-1, keepdims=True)
    return xf * lax.rsqrt(ms + EPS) * g


def _dot(a, b):
    return jnp.dot(a, b, preferred_element_type=F32)


def _dot_nt(a, b):
    return lax.dot_general(a, b, (((1,), (1,)), ((), ())), preferred_element_type=F32)


def _mla_proj_kernel(x_ref, posr_ref, g_ref, wd_ref, nq_ref, wuqt_ref, nkv_ref, wuk_ref, wuvt_ref,
                     invfc_ref, qt_ref, k_ref, vt_ref, *, scale):
    h = _rms(x_ref[0], g_ref[...]).astype(BF16)
    d = _dot(h, wd_ref[...])
    cq = _rms(d[:, :MLA_Q_RANK], nq_ref[...]).astype(BF16)
    ckv = _rms(d[:, MLA_Q_RANK:MLA_Q_RANK + MLA_KV_RANK], nkv_ref[...]).astype(BF16)
    kr = d[:, MLA_Q_RANK + MLA_KV_RANK:]

    half = MLA_ROPE // 2
    ang_t = invfc_ref[...] * posr_ref[0].astype(F32)
    cos_t = jnp.cos(ang_t)
    sin_t = jnp.sin(ang_t)

    kr_t = jnp.transpose(kr)
    k1, k2 = kr_t[MLA_NOPE:MLA_NOPE + half], kr_t[MLA_NOPE + half:MLA_NOPE + MLA_ROPE]
    tm = kr.shape[0]
    kr = jnp.transpose(jnp.concatenate(
        [jnp.zeros((MLA_NOPE, tm), F32), k1 * cos_t - k2 * sin_t, k2 * cos_t + k1 * sin_t,
         jnp.zeros((HEAD_SLAB - MLA_NOPE - MLA_ROPE, tm), F32)], axis=0))
    kf = _dot(ckv, wuk_ref[...])
    for hd in range(MLA_HEADS):
        sl = slice(hd * HEAD_SLAB, (hd + 1) * HEAD_SLAB)
        k_ref[0, :, sl] = (kf[:, sl] + kr).astype(BF16)
    vt_ref[0] = _dot_nt(wuvt_ref[...], ckv).astype(BF16)

    qft = _dot_nt(wuqt_ref[...], cq)
    cos_t = cos_t * scale
    sin_t = sin_t * scale
    for hd in range(MLA_HEADS):
        r0 = hd * HEAD_SLAB
        r1, r2, r3 = r0 + MLA_NOPE, r0 + MLA_NOPE + half, r0 + MLA_NOPE + MLA_ROPE
        x1, x2 = qft[r1:r2], qft[r2:r3]
        qt_ref[0, r0:r1] = (qft[r0:r1] * scale).astype(BF16)
        qt_ref[0, r1:r2] = (x1 * cos_t - x2 * sin_t).astype(BF16)
        qt_ref[0, r2:r3] = (x2 * cos_t + x1 * sin_t).astype(BF16)
        qt_ref[0, r3:r0 + HEAD_SLAB] = jnp.zeros((HEAD_SLAB - MLA_NOPE - MLA_ROPE, qft.shape[1]), BF16)


def _mla_proj(x, posr, g, wd, nq, wuqt, nkv, wuk, wuvt, invfc):
    b, s, dm = x.shape
    tm = TM_PROJ
    hw = MLA_HEADS * HEAD_SLAB
    vw = MLA_HEADS * MLA_V
    tok = lambda w: pl.BlockSpec((1, tm, w), lambda bi, si: (bi, si, 0))
    tok_t = lambda r: pl.BlockSpec((1, r, tm), lambda bi, si: (bi, 0, si))
    return pl.pallas_call(
        functools.partial(_mla_proj_kernel, scale=LOG2_E / math.sqrt(MLA_NOPE + MLA_ROPE)),
        grid=(b, s // tm),
        in_specs=[tok(dm), tok_t(1), _resident(g.shape), _resident(wd.shape), _resident(nq.shape),
                  _resident(wuqt.shape), _resident(nkv.shape), _resident(wuk.shape),
                  _resident(wuvt.shape), _resident(invfc.shape)],
        out_specs=[tok_t(hw), tok(hw), tok_t(vw)],
        out_shape=[jax.ShapeDtypeStruct((b, hw, s), BF16), jax.ShapeDtypeStruct((b, s, hw), BF16),
                   jax.ShapeDtypeStruct((b, vw, s), BF16)],
        compiler_params=_params("arbitrary", "arbitrary"),
        name="mla_proj",
    )(x, posr, g, wd, nq, wuqt, nkv, wuk, wuvt, invfc)


def _mla_attn_kernel(qt_ref, k_ref, vt_ref, o_ref, m_scr, l_scr, acc_scr, sa_scr, sb_scr, *,
                     tq, tk):
    qi = pl.program_id(2)
    m_scr[...] = jnp.full(m_scr.shape, -jnp.inf, F32)
    l_scr[...] = jnp.zeros(l_scr.shape, F32)
    acc_scr[...] = jnp.zeros(acc_scr.shape, F32)

    def scores(j, s_scr):
        ks = pl.multiple_of(j * tk, tk)
        for hd in range(2):
            sl = slice(hd * HEAD_SLAB, (hd + 1) * HEAD_SLAB)
            s_scr[hd] = _dot(k_ref[0, pl.ds(ks, tk), sl], qt_ref[0, sl, :])

    def finish(j, s_scr, diag=None):
        ks = pl.multiple_of(j * tk, tk)
        vtb = vt_ref[0, :, pl.ds(ks, tk)]
        chunks = [(0, tq, False)] if diag is None else [(c * tk, tk, c == diag) for c in range(diag, tq // tk)]
        ones = jnp.ones((BF16_ROWS, tk), BF16)
        for hd in range(2):
            rows = slice(hd * MLA_V, (hd + 1) * MLA_V)
            vt_ext = jnp.concatenate([vtb[rows, :], ones], axis=0)
            for q0, nq, triangular in chunks:
                cols = slice(q0, q0 + nq)
                st = s_scr[hd, :, cols]
                if triangular:
                    key = lax.broadcasted_iota(jnp.int32, st.shape, 0)
                    qry = lax.broadcasted_iota(jnp.int32, st.shape, 1)
                    st = jnp.where(key <= qry, st, -jnp.inf)
                m_old = m_scr[hd, :, cols]
                m_new = jnp.maximum(m_old, jnp.max(st, axis=0, keepdims=True))
                alpha = jnp.exp2(m_old - m_new)
                pv = _dot(vt_ext, jnp.exp2(st - m_new).astype(BF16))
                l_scr[hd, :, cols] = alpha * l_scr[hd, :, cols] + pv[MLA_V:MLA_V + 1]
                m_scr[hd, :, cols] = m_new
                acc_scr[rows, cols] = acc_scr[rows, cols] * alpha + pv[:MLA_V]

    span = tq // tk
    assert span % 2 == 0
    n_full = span * qi
    bufs = (sa_scr, sb_scr)
    scores(0, sa_scr)

    def body(t, carry):
        scores(2 * t + 1, sb_scr)
        finish(2 * t, sa_scr)
        scores(2 * t + 2, sa_scr)
        finish(2 * t + 1, sb_scr)
        return carry

    lax.fori_loop(0, n_full // 2, body, 0)
    for d in range(span):
        if d + 1 < span:
            scores(n_full + d + 1, bufs[(d + 1) % 2])
        finish(n_full + d, bufs[d % 2], diag=d)

    for hd in range(2):
        rows = slice(hd * MLA_V, (hd + 1) * MLA_V)
        acc_scr[rows, :] = acc_scr[rows, :] * (1.0 / l_scr[hd])
    o_ref[0] = jnp.transpose(acc_scr[...]).astype(BF16)


def _mla_attn(qt, k, vt):
    b, s, _ = k.shape
    tq, tk = TQ_MLA, TK_MLA
    pairs = MLA_HEADS // 2
    return pl.pallas_call(
        functools.partial(_mla_attn_kernel, tq=tq, tk=tk),
        grid=(b, pairs, s // tq),
        in_specs=[pl.BlockSpec((1, 2 * HEAD_SLAB, tq), lambda bi, pi, qi: (bi, pi, qi)),
                  pl.BlockSpec((1, s, 2 * HEAD_SLAB), lambda bi, pi, qi: (bi, 0, pi)),
                  pl.BlockSpec((1, 2 * MLA_V, s), lambda bi, pi, qi: (bi, pi, 0))],
        out_specs=pl.BlockSpec((1, tq, 2 * MLA_V), lambda bi, pi, qi: (bi, qi, pi)),
        out_shape=jax.ShapeDtypeStruct((b, s, MLA_HEADS * MLA_V), BF16),
        scratch_shapes=[pltpu.VMEM((2, 1, tq), F32), pltpu.VMEM((2, 1, tq), F32),
                        pltpu.VMEM((2 * MLA_V, tq), F32),
                        pltpu.VMEM((2, tk, tq), F32), pltpu.VMEM((2, tk, tq), F32)],
        compiler_params=_params("arbitrary", "arbitrary", "arbitrary"),
        name="mla_attn",
    )(qt, k, vt)


def _conv_kernel(x_ref, g_ref, win_ref, cw_ref, cb_ref, o_ref, tail_scr, *, tm, dm):
    @pl.when(pl.program_id(1) == 0)
    def _():
        tail_scr[...] = jnp.zeros(tail_scr.shape, F32)

    h = _rms(x_ref[0], g_ref[...]).astype(BF16)
    bcu = _dot(h, win_ref[...])
    gb = bcu[:, :dm]
    u = bcu[:, dm:2 * dm] * bcu[:, 2 * dm:]
    tail = tail_scr[...]
    row = lax.broadcasted_iota(jnp.int32, (tm, 1), 0)
    prev1 = jnp.where(row == 0, tail[7:8], pltpu.roll(u, 1, 0))
    prev2 = jnp.where(row == 0, tail[6:7], jnp.where(row == 1, tail[7:8], pltpu.roll(u, 2, 0)))
    y = cw_ref[0:1] * prev2 + cw_ref[1:2] * prev1 + cw_ref[2:3] * u + cb_ref[...]
    o_ref[0] = (gb * y).astype(BF16)
    tail_scr[...] = u[tm - 8:, :]


def _conv_pre(x, g, win, cw, cb):
    b, s, dm = x.shape
    tm = TM_PROJ
    tok = pl.BlockSpec((1, tm, dm), lambda bi, si: (bi, si, 0))
    return pl.pallas_call(
        functools.partial(_conv_kernel, tm=tm, dm=dm),
        grid=(b, s // tm),
        in_specs=[tok, _resident(g.shape), _resident(win.shape), _resident(cw.shape),
                  _resident(cb.shape)],
        out_specs=tok,
        out_shape=jax.ShapeDtypeStruct((b, s, dm), BF16),
        scratch_shapes=[pltpu.VMEM((8, dm), F32)],
        compiler_params=_params("arbitrary", "arbitrary"),
        name="conv_pre",
    )(x, g, win, cw, cb)


def _sb_proj_kernel(x_ref, g_ref, wk_ref, wqvt_ref, k_ref, qvt_ref, *, dm, scale):
    h = _rms(x_ref[0], g_ref[...]).astype(BF16)
    k_ref[0] = _dot(h, wk_ref[...]).astype(BF16)
    t = _dot_nt(wqvt_ref[...], h)
    qvt_ref[0, :dm] = (t[:dm] * scale).astype(BF16)
    qvt_ref[0, dm:] = t[dm:].astype(BF16)


def _sb_proj(x, g, wk, wqvt, scale):
    b, s, dm = x.shape
    tm = TM_PROJ
    return pl.pallas_call(
        functools.partial(_sb_proj_kernel, dm=dm, scale=scale),
        grid=(b, s // tm),
        in_specs=[pl.BlockSpec((1, tm, dm), lambda bi, si: (bi, si, 0)), _resident(g.shape),
                  _resident(wk.shape), _resident(wqvt.shape)],
        out_specs=[pl.BlockSpec((1, tm, dm), lambda bi, si: (bi, si, 0)),
                   pl.BlockSpec((1, 2 * dm, tm), lambda bi, si: (bi, 0, si))],
        out_shape=[jax.ShapeDtypeStruct((b, s, dm), BF16), jax.ShapeDtypeStruct((b, 2 * dm, s), BF16)],
        compiler_params=_params("arbitrary", "arbitrary"),
        name="sb_proj",
    )(x, g, wk, wqvt)


def _sb_attn_kernel(qt_ref, k_ref, vt_ref, ntri_ref, o_ref, run_scr, acc_scr, za_scr, zb_scr, qh_scr, *,
                    tq, tk):
    span = tq // tk
    assert span % 2 == 0
    qi = pl.program_id(2)
    last = span * (qi + 1) - 1
    hd_w = LANES // 2
    low = lax.broadcasted_iota(jnp.int32, (LANES, 1), 0) < hd_w
    qt = qt_ref[0]
    zero_q = jnp.zeros_like(qt)
    qh_scr[0] = jnp.where(low, qt, zero_q)
    qh_scr[1] = jnp.where(low, zero_q, qt)
    run_scr[...] = jnp.zeros(run_scr.shape, F32)
    acc_scr[...] = jnp.zeros(acc_scr.shape, F32)

    def scores(u, z_scr):
        ks = pl.multiple_of(jnp.maximum(last - u, 0) * tk, tk)
        kb = k_ref[0, pl.ds(ks, tk), :]
        for hd in range(2):
            z_scr[hd] = _dot(kb, qh_scr[hd])

    def finish(u, z_scr, diag=None):
        ks = pl.multiple_of((last - u) * tk, tk)
        vtb = vt_ref[0, :, pl.ds(ks, tk)]
        q0 = 0 if diag is None else diag * tk
        cols = slice(q0, tq)

        def masses(hd, triangular):
            z = z_scr[hd, :, cols]
            neg_abs = lax.bitcast_convert_type(
                lax.bitcast_convert_type(z, jnp.uint32) | jnp.uint32(0x80000000), F32)
            softplus = jnp.maximum(z, 0.0) + jnp.log2(1.0 + jnp.exp2(neg_abs))
            keep = None
            mass = softplus
            if triangular:
                keep = (lax.broadcasted_iota(jnp.int32, z.shape, 0)
                        < lax.broadcasted_iota(jnp.int32, z.shape, 1))
                mass = jnp.where(keep, softplus, 0.0)
            ntri = ntri_ref[...]
            sums = _dot(ntri, mass.astype(BF16))
            return z - softplus, keep, sums

        def weights(hd, log_beta, keep, sums):
            rows = slice(hd * hd_w, (hd + 1) * hd_w)
            a = jnp.exp2(log_beta + sums[:tk])
            if keep is not None:
                a = jnp.where(keep, a, 0.0)
            run = run_scr[hd, :, cols]
            acc_scr[rows, cols] += _dot(vtb[rows, :], a.astype(BF16)) * jnp.exp2(run)
            run_scr[hd, :, cols] = run + sums[tk:tk + 1]

        staged = [masses(hd, diag is not None) for hd in range(2)]
        for hd in range(2):
            weights(hd, *staged[hd])

    bufs = (za_scr, zb_scr)
    scores(0, za_scr)
    for u in range(span):
        scores(u + 1, bufs[(u + 1) % 2])
        finish(u, bufs[u % 2], diag=span - 1 - u)

    def body(t, carry):
        u = span + 2 * t
        scores(u + 1, zb_scr)
        finish(u, za_scr)
        scores(u + 2, za_scr)
        finish(u + 1, zb_scr)
        return carry

    lax.fori_loop(0, span * qi // 2, body, 0)
    o_ref[0] = jnp.transpose(acc_scr[...]).astype(BF16)


def _sb_attn(k, qvt, ntri):
    b, s, dm = k.shape
    tq, tk = TQ_SB, TK_SB
    pairs = SB_HEADS // 2
    return pl.pallas_call(
        functools.partial(_sb_attn_kernel, tq=tq, tk=tk),
        grid=(b, pairs, s // tq),
        in_specs=[pl.BlockSpec((1, LANES, tq), lambda bi, pi, qi: (bi, pi, qi)),
                  pl.BlockSpec((1, s, LANES), lambda bi, pi, qi: (bi, 0, pi)),
                  pl.BlockSpec((1, LANES, s), lambda bi, pi, qi: (bi, pairs + pi, 0)),
                  _resident(ntri.shape)],
        out_specs=pl.BlockSpec((1, tq, LANES), lambda bi, pi, qi: (bi, qi, pi)),
        out_shape=jax.ShapeDtypeStruct((b, s, dm), BF16),
        scratch_shapes=[pltpu.VMEM((2, 1, tq), F32), pltpu.VMEM((LANES, tq), F32),
                        pltpu.VMEM((2, tk, tq), F32), pltpu.VMEM((2, tk, tq), F32),
                        pltpu.VMEM((2, LANES, tq), BF16)],
        compiler_params=_params("arbitrary", "arbitrary", "arbitrary"),
        name="sb_attn",
    )(qvt, k, qvt, ntri)


def _mlp_kernel(x_ref, a_ref, wa_ref, g_ref, wup_ref, wdn_ref, gf_ref, o_ref, *, d_ff, final):
    x1 = x_ref[...] + _dot(a_ref[...], wa_ref[...])
    h = _rms(x1, g_ref[...]).astype(BF16)
    acc = x1
    for c in range(d_ff // FF_CHUNK):
        sl = slice(c * FF_CHUNK, (c + 1) * FF_CHUNK)
        r = jnp.maximum(_dot(h, wup_ref[:, sl]), 0.0)
        acc = acc + _dot((r * r).astype(BF16), wdn_ref[sl, :])
    if final:
        acc = _rms(acc, gf_ref[...])
    o_ref[...] = acc


def _mlp(x2, a2, wa, g, wup, wdn, gf, final):
    t, dm = x2.shape
    tm = TM_PROJ
    tok = pl.BlockSpec((tm, dm), lambda i: (i, 0))
    return pl.pallas_call(
        functools.partial(_mlp_kernel, d_ff=wup.shape[1], final=final),
        grid=(t // tm,),
        in_specs=[tok, tok, _resident(wa.shape), _resident(g.shape), _resident(wup.shape),
                  _resident(wdn.shape), _resident(gf.shape)],
        out_specs=tok,
        out_shape=jax.ShapeDtypeStruct((t, dm), F32),
        compiler_params=_params("arbitrary"),
        name="out_proj_mlp",
    )(x2, a2, wa, g, wup, wdn, gf)


def _row(v):
    return v.reshape(1, -1).astype(F32)


def _pad_heads(w, heads, width):
    k = w.shape[0]
    w = w.reshape(k, heads, width)
    return jnp.pad(w, ((0, 0), (0, 0), (0, HEAD_SLAB - width))).reshape(k, heads * HEAD_SLAB)


def _mla_layer(x, posr, invfc, norm_mix, w_dq, norm_q, w_uq, w_dkv, norm_kv, w_uk, w_uv):
    w_kr = jnp.pad(w_dkv[:, MLA_KV_RANK:], ((0, 0), (MLA_NOPE, HEAD_SLAB - MLA_NOPE - MLA_ROPE)))
    wd = jnp.concatenate([w_dq, w_dkv[:, :MLA_KV_RANK], w_kr], axis=1).astype(BF16)
    wuqt = _pad_heads(w_uq, MLA_HEADS, MLA_NOPE + MLA_ROPE).T.astype(BF16)
    wuk = _pad_heads(w_uk, MLA_HEADS, MLA_NOPE).astype(BF16)
    qt, k, vt = _mla_proj(x, posr, _row(norm_mix), wd, _row(norm_q), wuqt, _row(norm_kv), wuk,
                          w_uv.T.astype(BF16), invfc)
    return _mla_attn(qt, k, vt)


def kernel(x, positions, l0_norm_mix, l0_w_dq, l0_norm_q, l0_w_uq, l0_w_dkv, l0_norm_kv, l0_w_uk, l0_w_uv, l0_w_o, l0_norm_mlp, l0_w_up, l0_w_down, l1_norm_mix, l1_w_in, l1_conv_w, l1_conv_b, l1_w_out, l1_norm_mlp, l1_w_up, l1_w_down, l2_norm_mix, l2_w_qkv, l2_w_o, l2_norm_mlp, l2_w_up, l2_w_down, l3_norm_mix, l3_w_dq, l3_norm_q, l3_w_uq, l3_w_dkv, l3_norm_kv, l3_w_uk, l3_w_uv, l3_w_o, l3_norm_mlp, l3_w_up, l3_w_down, final_norm):
    b, s, dm = x.shape
    t = b * s
    posr = positions.reshape(b, 1, s)
    inv_freq = ROPE_THETA ** (-jnp.arange(0, MLA_ROPE, 2, dtype=F32) / MLA_ROPE)
    invfc = inv_freq.reshape(-1, 1)
    tri_row = jnp.arange(TK_SB + BF16_ROWS)[:, None]
    ntri = -((jnp.arange(TK_SB)[None, :] > tri_row) | (tri_row >= TK_SB)).astype(BF16)
    gf = _row(final_norm)

    def mlp(xc, a, wa, g, wup, wdn, final=False):
        out = _mlp(xc.reshape(t, dm), a.reshape(t, dm), wa.astype(BF16), _row(g), wup.astype(BF16),
                   wdn.astype(BF16), gf, final)
        return out.reshape(b, s, dm)

    o = _mla_layer(x, posr, invfc, l0_norm_mix, l0_w_dq, l0_norm_q, l0_w_uq, l0_w_dkv, l0_norm_kv,
                   l0_w_uk, l0_w_uv)
    x = mlp(x, o, l0_w_o, l0_norm_mlp, l0_w_up, l0_w_down)

    gy = _conv_pre(x, _row(l1_norm_mix), l1_w_in.astype(BF16), l1_conv_w.astype(F32), _row(l1_conv_b))
    x = mlp(x, gy, l1_w_out, l1_norm_mlp, l1_w_up, l1_w_down)

    w_qv = jnp.concatenate([l2_w_qkv[:, :dm], l2_w_qkv[:, 2 * dm:]], axis=1)
    k, qvt = _sb_proj(x, _row(l2_norm_mix), l2_w_qkv[:, dm:2 * dm].astype(BF16), w_qv.T.astype(BF16),
                      LOG2_E / math.sqrt(dm // SB_HEADS))
    o = _sb_attn(k, qvt, ntri)
    x = mlp(x, o, l2_w_o, l2_norm_mlp, l2_w_up, l2_w_down)

    o = _mla_layer(x, posr, invfc, l3_norm_mix, l3_w_dq, l3_norm_q, l3_w_uq, l3_w_dkv, l3_norm_kv,
                   l3_w_uk, l3_w_uv)
    return mlp(x, o, l3_w_o, l3_norm_mlp, l3_w_up, l3_w_down, final=True)
```

```python
import functools
import math

import jax
import jax.numpy as jnp
from jax import lax
from jax.experimental import pallas as pl
from jax.experimental.pallas import tpu as pltpu

F32 = jnp.float32
BF16 = jnp.bfloat16

EPS = 1e-6
ROPE_THETA = 10000.0
MLA_HEADS = 16
MLA_NOPE = 64
MLA_ROPE = 32
MLA_V = 64
MLA_Q_RANK = 384
MLA_KV_RANK = 256
SB_HEADS = 16

LANES = 128
BF16_ROWS = 16
HEAD_SLAB = LANES
VMEM_LIMIT = 56 * 1024 * 1024

TM_PROJ = 512
TQ_MLA = 2048
TK_MLA = 512
TQ_SB = 1024
TK_SB = 256
LOG2_E = math.log2(math.e)
FF_CHUNK = 1024


def _params(*sem):
    return pltpu.CompilerParams(dimension_semantics=sem, vmem_limit_bytes=VMEM_LIMIT)


def _resident(shape):
    nd = len(shape)
    return pl.BlockSpec(shape, lambda *_: (0,) * nd, pipeline_mode=pl.Buffered(1))


def _rms(xf, g):
    ms = jnp.mean(xf * xf, axis=-1, keepdims=True)
    return xf * lax.rsqrt(ms + EPS) * g


def _dot(a, b):
    return jnp.dot(a, b, preferred_element_type=F32)


def _dot_nt(a, b):
    return lax.dot_general(a, b, (((1,), (1,)), ((), ())), preferred_element_type=F32)


def _mla_proj_kernel(x_ref, posr_ref, g_ref, wd_ref, nq_ref, wuqt_ref, nkv_ref, wuk_ref, wuvt_ref,
                     invfc_ref, qt_ref, k_ref, vt_ref, *, scale):
    h = _rms(x_ref[0], g_ref[...]).astype(BF16)
    d = _dot(h, wd_ref[...])
    cq = _rms(d[:, :MLA_Q_RANK], nq_ref[...]).astype(BF16)
    ckv = _rms(d[:, MLA_Q_RANK:MLA_Q_RANK + MLA_KV_RANK], nkv_ref[...]).astype(BF16)
    kr = d[:, MLA_Q_RANK + MLA_KV_RANK:]

    half = MLA_ROPE // 2
    ang_t = invfc_ref[...] * posr_ref[0].astype(F32)
    cos_t = jnp.cos(ang_t)
    sin_t = jnp.sin(ang_t)

    kr_t = jnp.transpose(kr)
    k1, k2 = kr_t[MLA_NOPE:MLA_NOPE + half], kr_t[MLA_NOPE + half:MLA_NOPE + MLA_ROPE]
    tm = kr.shape[0]
    kr = jnp.transpose(jnp.concatenate(
        [jnp.zeros((MLA_NOPE, tm), F32), k1 * cos_t - k2 * sin_t, k2 * cos_t + k1 * sin_t,
         jnp.zeros((HEAD_SLAB - MLA_NOPE - MLA_ROPE, tm), F32)], axis=0))
    kf = _dot(ckv, wuk_ref[...])
    for hd in range(MLA_HEADS):
        sl = slice(hd * HEAD_SLAB, (hd + 1) * HEAD_SLAB)
        k_ref[0, :, sl] = (kf[:, sl] + kr).astype(BF16)
    vt_ref[0] = _dot_nt(wuvt_ref[...], ckv).astype(BF16)

    qft = _dot_nt(wuqt_ref[...], cq)
    cos_t = cos_t * scale
    sin_t = sin_t * scale
    for hd in range(MLA_HEADS):
        r0 = hd * HEAD_SLAB
        r1, r2, r3 = r0 + MLA_NOPE, r0 + MLA_NOPE + half, r0 + MLA_NOPE + MLA_ROPE
        x1, x2 = qft[r1:r2], qft[r2:r3]
        qt_ref[0, r0:r1] = (qft[r0:r1] * scale).astype(BF16)
        qt_ref[0, r1:r2] = (x1 * cos_t - x2 * sin_t).astype(BF16)
        qt_ref[0, r2:r3] = (x2 * cos_t + x1 * sin_t).astype(BF16)
        qt_ref[0, r3:r0 + HEAD_SLAB] = jnp.zeros((HEAD_SLAB - MLA_NOPE - MLA_ROPE, qft.shape[1]), BF16)


def _mla_proj(x, posr, g, wd, nq, wuqt, nkv, wuk, wuvt, invfc):
    b, s, dm = x.shape
    tm = TM_PROJ
    hw = MLA_HEADS * HEAD_SLAB
    vw = MLA_HEADS * MLA_V
    tok = lambda w: pl.BlockSpec((1, tm, w), lambda bi, si: (bi, si, 0))
    tok_t = lambda r: pl.BlockSpec((1, r, tm), lambda bi, si: (bi, 0, si))
    return pl.pallas_call(
        functools.partial(_mla_proj_kernel, scale=LOG2_E / math.sqrt(MLA_NOPE + MLA_ROPE)),
        grid=(b, s // tm),
        in_specs=[tok(dm), tok_t(1), _resident(g.shape), _resident(wd.shape), _resident(nq.shape),
                  _resident(wuqt.shape), _resident(nkv.shape), _resident(wuk.shape),
                  _resident(wuvt.shape), _resident(invfc.shape)],
        out_specs=[tok_t(hw), tok(hw), tok_t(vw)],
        out_shape=[jax.ShapeDtypeStruct((b, hw, s), BF16), jax.ShapeDtypeStruct((b, s, hw), BF16),
                   jax.ShapeDtypeStruct((b, vw, s), BF16)],
        compiler_params=_params("arbitrary", "arbitrary"),
        name="mla_proj",
    )(x, posr, g, wd, nq, wuqt, nkv, wuk, wuvt, invfc)


def _mla_attn_kernel(qt_ref, k_ref, vt_ref, o_ref, m_scr, l_scr, acc_scr, sa_scr, sb_scr, *,
                     tq, tk):
    qi = pl.program_id(2)
    m_scr[...] = jnp.full(m_scr.shape, -jnp.inf, F32)
    l_scr[...] = jnp.zeros(l_scr.shape, F32)
    acc_scr[...] = jnp.zeros(acc_scr.shape, F32)

    def scores(j, s_scr):
        ks = pl.multiple_of(j * tk, tk)
        for hd in range(2):
            sl = slice(hd * HEAD_SLAB, (hd + 1) * HEAD_SLAB)
            s_scr[hd] = _dot(k_ref[0, pl.ds(ks, tk), sl], qt_ref[0, sl, :])

    def finish(j, s_scr, diag=None):
        ks = pl.multiple_of(j * tk, tk)
        vtb = vt_ref[0, :, pl.ds(ks, tk)]
        chunks = [(0, tq, False)] if diag is None else [(c * tk, tk, c == diag) for c in range(diag, tq // tk)]
        ones = jnp.ones((BF16_ROWS, tk), BF16)
        for hd in range(2):
            rows = slice(hd * MLA_V, (hd + 1) * MLA_V)
            vt_ext = jnp.concatenate([vtb[rows, :], ones], axis=0)
            for q0, nq, triangular in chunks:
                cols = slice(q0, q0 + nq)
                st = s_scr[hd, :, cols]
                if triangular:
                    key = lax.broadcasted_iota(jnp.int32, st.shape, 0)
                    qry = lax.broadcasted_iota(jnp.int32, st.shape, 1)
                    st = jnp.where(key <= qry, st, -jnp.inf)
                m_old = m_scr[hd, :, cols]
                m_new = jnp.maximum(m_old, jnp.max(st, axis=0, keepdims=True))
                alpha = jnp.exp2(m_old - m_new)
                pv = _dot(vt_ext, jnp.exp2(st - m_new).astype(BF16))
                l_scr[hd, :, cols] = alpha * l_scr[hd, :, cols] + pv[MLA_V:MLA_V + 1]
                m_scr[hd, :, cols] = m_new
                acc_scr[rows, cols] = acc_scr[rows, cols] * alpha + pv[:MLA_V]

    span = tq // tk
    assert span % 2 == 0
    n_full = span * qi
    bufs = (sa_scr, sb_scr)
    scores(0, sa_scr)

    def body(t, carry):
        scores(2 * t + 1, sb_scr)
        finish(2 * t, sa_scr)
        scores(2 * t + 2, sa_scr)
        finish(2 * t + 1, sb_scr)
        return carry

    lax.fori_loop(0, n_full // 2, body, 0)
    for d in range(span):
        if d + 1 < span:
            scores(n_full + d + 1, bufs[(d + 1) % 2])
        finish(n_full + d, bufs[d % 2], diag=d)

    for hd in range(2):
        rows = slice(hd * MLA_V, (hd + 1) * MLA_V)
        acc_scr[rows, :] = acc_scr[rows, :] * (1.0 / l_scr[hd])
    o_ref[0] = jnp.transpose(acc_scr[...]).astype(BF16)


def _mla_attn(qt, k, vt):
    b, s, _ = k.shape
    tq, tk = TQ_MLA, TK_MLA
    pairs = MLA_HEADS // 2
    return pl.pallas_call(
        functools.partial(_mla_attn_kernel, tq=tq, tk=tk),
        grid=(b, pairs, s // tq),
        in_specs=[pl.BlockSpec((1, 2 * HEAD_SLAB, tq), lambda bi, pi, qi: (bi, pi, qi)),
                  pl.BlockSpec((1, s, 2 * HEAD_SLAB), lambda bi, pi, qi: (bi, 0, pi)),
                  pl.BlockSpec((1, 2 * MLA_V, s), lambda bi, pi, qi: (bi, pi, 0))],
        out_specs=pl.BlockSpec((1, tq, 2 * MLA_V), lambda bi, pi, qi: (bi, qi, pi)),
        out_shape=jax.ShapeDtypeStruct((b, s, MLA_HEADS * MLA_V), BF16),
        scratch_shapes=[pltpu.VMEM((2, 1, tq), F32), pltpu.VMEM((2, 1, tq), F32),
                        pltpu.VMEM((2 * MLA_V, tq), F32),
                        pltpu.VMEM((2, tk, tq), F32), pltpu.VMEM((2, tk, tq), F32)],
        compiler_params=_params("arbitrary", "arbitrary", "arbitrary"),
        name="mla_attn",
    )(qt, k, vt)


def _conv_kernel(x_ref, g_ref, win_ref, cw_ref, cb_ref, o_ref, tail_scr, *, tm, dm):
    @pl.when(pl.program_id(1) == 0)
    def _():
        tail_scr[...] = jnp.zeros(tail_scr.shape, F32)

    h = _rms(x_ref[0], g_ref[...]).astype(BF16)
    bcu = _dot(h, win_ref[...])
    gb = bcu[:, :dm]
    u = bcu[:, dm:2 * dm] * bcu[:, 2 * dm:]
    tail = tail_scr[...]
    row = lax.broadcasted_iota(jnp.int32, (tm, 1), 0)
    prev1 = jnp.where(row == 0, tail[7:8], pltpu.roll(u, 1, 0))
    prev2 = jnp.where(row == 0, tail[6:7], jnp.where(row == 1, tail[7:8], pltpu.roll(u, 2, 0)))
    y = cw_ref[0:1] * prev2 + cw_ref[1:2] * prev1 + cw_ref[2:3] * u + cb_ref[...]
    o_ref[0] = (gb * y).astype(BF16)
    tail_scr[...] = u[tm - 8:, :]


def _conv_pre(x, g, win, cw, cb):
    b, s, dm = x.shape
    tm = TM_PROJ
    tok = pl.BlockSpec((1, tm, dm), lambda bi, si: (bi, si, 0))
    return pl.pallas_call(
        functools.partial(_conv_kernel, tm=tm, dm=dm),
        grid=(b, s // tm),
        in_specs=[tok, _resident(g.shape), _resident(win.shape), _resident(cw.shape),
                  _resident(cb.shape)],
        out_specs=tok,
        out_shape=jax.ShapeDtypeStruct((b, s, dm), BF16),
        scratch_shapes=[pltpu.VMEM((8, dm), F32)],
        compiler_params=_params("arbitrary", "arbitrary"),
        name="conv_pre",
    )(x, g, win, cw, cb)


def _sb_proj_kernel(x_ref, g_ref, wk_ref, wqvt_ref, k_ref, qvt_ref, *, dm, scale):
    h = _rms(x_ref[0], g_ref[...]).astype(BF16)
    k_ref[0] = _dot(h, wk_ref[...]).astype(BF16)
    t = _dot_nt(wqvt_ref[...], h)
    qvt_ref[0, :dm] = (t[:dm] * scale).astype(BF16)
    qvt_ref[0, dm:] = t[dm:].astype(BF16)


def _sb_proj(x, g, wk, wqvt, scale):
    b, s, dm = x.shape
    tm = TM_PROJ
    return pl.pallas_call(
        functools.partial(_sb_proj_kernel, dm=dm, scale=scale),
        grid=(b, s // tm),
        in_specs=[pl.BlockSpec((1, tm, dm), lambda bi, si: (bi, si, 0)), _resident(g.shape),
                  _resident(wk.shape), _resident(wqvt.shape)],
        out_specs=[pl.BlockSpec((1, tm, dm), lambda bi, si: (bi, si, 0)),
                   pl.BlockSpec((1, 2 * dm, tm), lambda bi, si: (bi, 0, si))],
        out_shape=[jax.ShapeDtypeStruct((b, s, dm), BF16), jax.ShapeDtypeStruct((b, 2 * dm, s), BF16)],
        compiler_params=_params("arbitrary", "arbitrary"),
        name="sb_proj",
    )(x, g, wk, wqvt)


def _sb_attn_kernel(qt_ref, k_ref, vt_ref, ntri_ref, o_ref, run_scr, acc_scr, za_scr, zb_scr, qh_scr, *,
                    tq, tk):
    span = tq // tk
    assert span % 2 == 0
    qi = pl.program_id(2)
    last = span * (qi + 1) - 1
    hd_w = LANES // 2
    low = lax.broadcasted_iota(jnp.int32, (LANES, 1), 0) < hd_w
    qt = qt_ref[0]
    zero_q = jnp.zeros_like(qt)
    qh_scr[0] = jnp.where(low, qt, zero_q)
    qh_scr[1] = jnp.where(low, zero_q, qt)
    run_scr[...] = jnp.zeros(run_scr.shape, F32)
    acc_scr[...] = jnp.zeros(acc_scr.shape, F32)

    def scores(u, z_scr):
        ks = pl.multiple_of(jnp.maximum(last - u, 0) * tk, tk)
        kb = k_ref[0, pl.ds(ks, tk), :]
        for hd in range(2):
            z_scr[hd] = _dot(kb, qh_scr[hd])

    def finish(u, z_scr, diag=None):
        ks = pl.multiple_of((last - u) * tk, tk)
        vtb = vt_ref[0, :, pl.ds(ks, tk)]
        q0 = 0 if diag is None else diag * tk
        cols = slice(q0, tq)

        def masses(hd, triangular):
            z = z_scr[hd, :, cols]
            neg_abs = lax.bitcast_convert_type(
                lax.bitcast_convert_type(z, jnp.uint32) | jnp.uint32(0x80000000), F32)
            softplus = jnp.maximum(z, 0.0) + jnp.log2(1.0 + jnp.exp2(neg_abs))
            keep = None
            mass = softplus
            if triangular:
                keep = (lax.broadcasted_iota(jnp.int32, z.shape, 0)
                        < lax.broadcasted_iota(jnp.int32, z.shape, 1))
                mass = jnp.where(keep, softplus, 0.0)
            ntri = ntri_ref[...]
            sums = _dot(ntri, mass.astype(BF16))
            return z - softplus, keep, sums

        def weights(hd, log_beta, keep, sums):
            rows = slice(hd * hd_w, (hd + 1) * hd_w)
            a = jnp.exp2(log_beta + sums[:tk])
            if keep is not None:
                a = jnp.where(keep, a, 0.0)
            run = run_scr[hd, :, cols]
            acc_scr[rows, cols] += _dot(vtb[rows, :], a.astype(BF16)) * jnp.exp2(run)
            run_scr[hd, :, cols] = run + sums[tk:tk + 1]

        staged = [masses(hd, diag is not None) for hd in range(2)]
        for hd in range(2):
            weights(hd, *staged[hd])

    bufs = (za_scr, zb_scr)
    scores(0, za_scr)
    for u in range(span):
        scores(u + 1, bufs[(u + 1) % 2])
        finish(u, bufs[u % 2], diag=span - 1 - u)

    def body(t, carry):
        u = span + 2 * t
        scores(u + 1, zb_scr)
        finish(u, za_scr)
        scores(u + 2, za_scr)
        finish(u + 1, zb_scr)
        return carry

    lax.fori_loop(0, span * qi // 2, body, 0)
    o_ref[0] = jnp.transpose(acc_scr[...]).astype(BF16)


def _sb_attn(k, qvt, ntri):
    b, s, dm = k.shape
    tq, tk = TQ_SB, TK_SB
    pairs = SB_HEADS // 2
    return pl.pallas_call(
        functools.partial(_sb_attn_kernel, tq=tq, tk=tk),
        grid=(b, pairs, s // tq),
        in_specs=[pl.BlockSpec((1, LANES, tq), lambda bi, pi, qi: (bi, pi, qi)),
                  pl.BlockSpec((1, s, LANES), lambda bi, pi, qi: (bi, 0, pi)),
                  pl.BlockSpec((1, LANES, s), lambda bi, pi, qi: (bi, pairs + pi, 0)),
                  _resident(ntri.shape)],
        out_specs=pl.BlockSpec((1, tq, LANES), lambda bi, pi, qi: (bi, qi, pi)),
        out_shape=jax.ShapeDtypeStruct((b, s, dm), BF16),
        scratch_shapes=[pltpu.VMEM((2, 1, tq), F32), pltpu.VMEM((LANES, tq), F32),
                        pltpu.VMEM((2, tk, tq), F32), pltpu.VMEM((2, tk, tq), F32),
                        pltpu.VMEM((2, LANES, tq), BF16)],
        compiler_params=_params("arbitrary", "arbitrary", "arbitrary"),
        name="sb_attn",
    )(qvt, k, qvt, ntri)


def _mlp_kernel(x_ref, a_ref, wa_ref, g_ref, wup_ref, wdn_ref, gf_ref, o_ref, *, d_ff, final):
    x1 = x_ref[...] + _dot(a_ref[...], wa_ref[...])
    h = _rms(x1, g_ref[...]).astype(BF16)
    acc = x1
    for c in range(d_ff // FF_CHUNK):
        sl = slice(c * FF_CHUNK, (c + 1) * FF_CHUNK)
        r = jnp.maximum(_dot(h, wup_ref[:, sl]), 0.0)
        acc = acc + _dot((r * r).astype(BF16), wdn_ref[sl, :])
    if final:
        acc = _rms(acc, gf_ref[...])
    o_ref[...] = acc


def _mlp(x2, a2, wa, g, wup, wdn, gf, final):
    t, dm = x2.shape
    tm = TM_PROJ
    tok = pl.BlockSpec((tm, dm), lambda i: (i, 0))
    return pl.pallas_call(
        functools.partial(_mlp_kernel, d_ff=wup.shape[1], final=final),
        grid=(t // tm,),
        in_specs=[tok, tok, _resident(wa.shape), _resident(g.shape), _resident(wup.shape),
                  _resident(wdn.shape), _resident(gf.shape)],
        out_specs=tok,
        out_shape=jax.ShapeDtypeStruct((t, dm), F32),
        compiler_params=_params("arbitrary"),
        name="out_proj_mlp",
    )(x2, a2, wa, g, wup, wdn, gf)


def _row(v):
    return v.reshape(1, -1).astype(F32)


def _pad_heads(w, heads, width):
    k = w.shape[0]
    w = w.reshape(k, heads, width)
    return jnp.pad(w, ((0, 0), (0, 0), (0, HEAD_SLAB - width))).reshape(k, heads * HEAD_SLAB)


def _mla_layer(x, posr, invfc, norm_mix, w_dq, norm_q, w_uq, w_dkv, norm_kv, w_uk, w_uv):
    w_kr = jnp.pad(w_dkv[:, MLA_KV_RANK:], ((0, 0), (MLA_NOPE, HEAD_SLAB - MLA_NOPE - MLA_ROPE)))
    wd = jnp.concatenate([w_dq, w_dkv[:, :MLA_KV_RANK], w_kr], axis=1).astype(BF16)
    wuqt = _pad_heads(w_uq, MLA_HEADS, MLA_NOPE + MLA_ROPE).T.astype(BF16)
    wuk = _pad_heads(w_uk, MLA_HEADS, MLA_NOPE).astype(BF16)
    qt, k, vt = _mla_proj(x, posr, _row(norm_mix), wd, _row(norm_q), wuqt, _row(norm_kv), wuk,
                          w_uv.T.astype(BF16), invfc)
    return _mla_attn(qt, k, vt)


def kernel(x, positions, l0_norm_mix, l0_w_dq, l0_norm_q, l0_w_uq, l0_w_dkv, l0_norm_kv, l0_w_uk, l0_w_uv, l0_w_o, l0_norm_mlp, l0_w_up, l0_w_down, l1_norm_mix, l1_w_in, l1_conv_w, l1_conv_b, l1_w_out, l1_norm_mlp, l1_w_up, l1_w_down, l2_norm_mix, l2_w_qkv, l2_w_o, l2_norm_mlp, l2_w_up, l2_w_down, l3_norm_mix, l3_w_dq, l3_norm_q, l3_w_uq, l3_w_dkv, l3_norm_kv, l3_w_uk, l3_w_uv, l3_w_o, l3_norm_mlp, l3_w_up, l3_w_down, final_norm):
    b, s, dm = x.shape
    t = b * s
    posr = positions.reshape(b, 1, s)
    inv_freq = ROPE_THETA ** (-jnp.arange(0, MLA_ROPE, 2, dtype=F32) / MLA_ROPE)
    invfc = inv_freq.reshape(-1, 1)
    tri_row = jnp.arange(TK_SB + BF16_ROWS)[:, None]
    ntri = -((jnp.arange(TK_SB)[None, :] > tri_row) | (tri_row >= TK_SB)).astype(BF16)
    gf = _row(final_norm)

    def mlp(xc, a, wa, g, wup, wdn, final=False):
        out = _mlp(xc.reshape(t, dm), a.reshape(t, dm), wa.astype(BF16), _row(g), wup.astype(BF16),
                   wdn.astype(BF16), gf, final)
        return out.reshape(b, s, dm)

    o = _mla_layer(x, posr, invfc, l0_norm_mix, l0_w_dq, l0_norm_q, l0_w_uq, l0_w_dkv, l0_norm_kv,
                   l0_w_uk, l0_w_uv)
    x = mlp(x, o, l0_w_o, l0_norm_mlp, l0_w_up, l0_w_down)

    gy = _conv_pre(x, _row(l1_norm_mix), l1_w_in.astype(BF16), l1_conv_w.astype(F32), _row(l1_conv_b))
    x = mlp(x, gy, l1_w_out, l1_norm_mlp, l1_w_up, l1_w_down)

    w_qv = jnp.concatenate([l2_w_qkv[:, :dm], l2_w_qkv[:, 2 * dm:]], axis=1)
    k, qvt = _sb_proj(x, _row(l2_norm_mix), l2_w_qkv[:, dm:2 * dm].astype(BF16), w_qv.T.astype(BF16),
                      LOG2_E / math.sqrt(dm // SB_HEADS))
    o = _sb_attn(k, qvt, ntri)
    x = mlp(x, o, l2_w_o, l2_norm_mlp, l2_w_up, l2_w_down)

    o = _mla_layer(x, posr, invfc, l3_norm_mix, l3_w_dq, l3_norm_q, l3_w_uq, l3_w_dkv, l3_norm_kv,
                   l3_w_uk, l3_w_uv)
    return mlp(x, o, l3_w_o, l3_norm_mlp, l3_w_up, l3_w_down, final=True)
```

```python
import functools
import math

import jax
import jax.numpy as jnp
from jax import lax
from jax.experimental import pallas as pl
from jax.experimental.pallas import tpu as pltpu

F32 = jnp.float32
BF16 = jnp.bfloat16

EPS = 1e-6
ROPE_THETA = 10000.0
MLA_HEADS = 16
MLA_NOPE = 64
MLA_ROPE = 32
MLA_V = 64
MLA_Q_RANK = 384
MLA_KV_RANK = 256
SB_HEADS = 16

LANES = 128
BF16_ROWS = 16
HEAD_SLAB = LANES
VMEM_LIMIT = 56 * 1024 * 1024

TM_PROJ = 512
TQ_MLA = 2048
TK_MLA = 512
TQ_SB = 1024
TK_SB = 256
LOG2_E = math.log2(math.e)
FF_CHUNK = 1024


def _params(*sem):
    return pltpu.CompilerParams(dimension_semantics=sem, vmem_limit_bytes=VMEM_LIMIT)


def _resident(shape):
    nd = len(shape)
    return pl.BlockSpec(shape, lambda *_: (0,) * nd, pipeline_mode=pl.Buffered(1))


def _rms(xf, g):
    ms = jnp.mean(xf * xf, axis=-1, keepdims=True)
    return xf * lax.rsqrt(ms + EPS) * g


def _dot(a, b):
    return jnp.dot(a, b, preferred_element_type=F32)


def _dot_nt(a, b):
    return lax.dot_general(a, b, (((1,), (1,)), ((), ())), preferred_element_type=F32)


def _mla_proj_kernel(x_ref, posr_ref, g_ref, wd_ref, nq_ref, wuqt_ref, nkv_ref, wuk_ref, wuvt_ref,
                     invfc_ref, qt_ref, k_ref, vt_ref, *, scale):
    h = _rms(x_ref[0], g_ref[...]).astype(BF16)
    d = _dot(h, wd_ref[...])
    cq = _rms(d[:, :MLA_Q_RANK], nq_ref[...]).astype(BF16)
    ckv = _rms(d[:, MLA_Q_RANK:MLA_Q_RANK + MLA_KV_RANK], nkv_ref[...]).astype(BF16)
    kr = d[:, MLA_Q_RANK + MLA_KV_RANK:]

    half = MLA_ROPE // 2
    ang_t = invfc_ref[...] * posr_ref[0].astype(F32)
    cos_t = jnp.cos(ang_t)
    sin_t = jnp.sin(ang_t)

    kr_t = jnp.transpose(kr)
    k1, k2 = kr_t[MLA_NOPE:MLA_NOPE + half], kr_t[MLA_NOPE + half:MLA_NOPE + MLA_ROPE]
    tm = kr.shape[0]
    kr = jnp.transpose(jnp.concatenate(
        [jnp.zeros((MLA_NOPE, tm), F32), k1 * cos_t - k2 * sin_t, k2 * cos_t + k1 * sin_t,
         jnp.zeros((HEAD_SLAB - MLA_NOPE - MLA_ROPE, tm), F32)], axis=0))
    kf = _dot(ckv, wuk_ref[...])
    for hd in range(MLA_HEADS):
        sl = slice(hd * HEAD_SLAB, (hd + 1) * HEAD_SLAB)
        k_ref[0, :, sl] = (kf[:, sl] + kr).astype(BF16)
    vt_ref[0] = _dot_nt(wuvt_ref[...], ckv).astype(BF16)

    qft = _dot_nt(wuqt_ref[...], cq)
    cos_t = cos_t * scale
    sin_t = sin_t * scale
    for hd in range(MLA_HEADS):
        r0 = hd * HEAD_SLAB
        r1, r2, r3 = r0 + MLA_NOPE, r0 + MLA_NOPE + half, r0 + MLA_NOPE + MLA_ROPE
        x1, x2 = qft[r1:r2], qft[r2:r3]
        qt_ref[0, r0:r1] = (qft[r0:r1] * scale).astype(BF16)
        qt_ref[0, r1:r2] = (x1 * cos_t - x2 * sin_t).astype(BF16)
        qt_ref[0, r2:r3] = (x2 * cos_t + x1 * sin_t).astype(BF16)
        qt_ref[0, r3:r0 + HEAD_SLAB] = jnp.zeros((HEAD_SLAB - MLA_NOPE - MLA_ROPE, qft.shape[1]), BF16)


def _mla_proj(x, posr, g, wd, nq, wuqt, nkv, wuk, wuvt, invfc):
    b, s, dm = x.shape
    tm = TM_PROJ
    hw = MLA_HEADS * HEAD_SLAB
    vw = MLA_HEADS * MLA_V
    tok = lambda w: pl.BlockSpec((1, tm, w), lambda bi, si: (bi, si, 0))
    tok_t = lambda r: pl.BlockSpec((1, r, tm), lambda bi, si: (bi, 0, si))
    return pl.pallas_call(
        functools.partial(_mla_proj_kernel, scale=LOG2_E / math.sqrt(MLA_NOPE + MLA_ROPE)),
        grid=(b, s // tm),
        in_specs=[tok(dm), tok_t(1), _resident(g.shape), _resident(wd.shape), _resident(nq.shape),
                  _resident(wuqt.shape), _resident(nkv.shape), _resident(wuk.shape),
                  _resident(wuvt.shape), _resident(invfc.shape)],
        out_specs=[tok_t(hw), tok(hw), tok_t(vw)],
        out_shape=[jax.ShapeDtypeStruct((b, hw, s), BF16), jax.ShapeDtypeStruct((b, s, hw), BF16),
                   jax.ShapeDtypeStruct((b, vw, s), BF16)],
        compiler_params=_params("arbitrary", "arbitrary"),
        name="mla_proj",
    )(x, posr, g, wd, nq, wuqt, nkv, wuk, wuvt, invfc)


def _mla_attn_kernel(qt_ref, k_ref, vt_ref, o_ref, m_scr, l_scr, acc_scr, sa_scr, sb_scr, *,
                     tq, tk):
    qi = pl.program_id(2)
    m_scr[...] = jnp.full(m_scr.shape, -jnp.inf, F32)
    l_scr[...] = jnp.zeros(l_scr.shape, F32)
    acc_scr[...] = jnp.zeros(acc_scr.shape, F32)

    def scores(j, s_scr, q0=0):
        ks = pl.multiple_of(j * tk, tk)
        for hd in range(2):
            sl = slice(hd * HEAD_SLAB, (hd + 1) * HEAD_SLAB)
            s_scr[hd, :, q0:] = _dot(k_ref[0, pl.ds(ks, tk), sl], qt_ref[0, sl, q0:])

    def finish(j, s_scr, diag=None):
        ks = pl.multiple_of(j * tk, tk)
        vtb = vt_ref[0, :, pl.ds(ks, tk)]
        chunks = [(0, tq, False)] if diag is None else [(c * tk, tk, c == diag) for c in range(diag, tq // tk)]
        ones = jnp.ones((BF16_ROWS, tk), BF16)
        for hd in range(2):
            rows = slice(hd * MLA_V, (hd + 1) * MLA_V)
            vt_ext = jnp.concatenate([vtb[rows, :], ones], axis=0)
            for q0, nq, triangular in chunks:
                cols = slice(q0, q0 + nq)
                st = s_scr[hd, :, cols]
                if triangular:
                    key = lax.broadcasted_iota(jnp.int32, st.shape, 0)
                    qry = lax.broadcasted_iota(jnp.int32, st.shape, 1)
                    st = jnp.where(key <= qry, st, -jnp.inf)
                m_old = m_scr[hd, :, cols]
                m_new = jnp.maximum(m_old, jnp.max(st, axis=0, keepdims=True))
                alpha = jnp.exp2(m_old - m_new)
                pv = _dot(vt_ext, jnp.exp2(st - m_new).astype(BF16))
                l_scr[hd, :, cols] = alpha * l_scr[hd, :, cols] + pv[MLA_V:MLA_V + 1]
                m_scr[hd, :, cols] = m_new
                acc_scr[rows, cols] = acc_scr[rows, cols] * alpha + pv[:MLA_V]

    span = tq // tk
    assert span % 2 == 0
    n_full = span * qi
    bufs = (sa_scr, sb_scr)
    scores(0, sa_scr)

    def body(t, carry):
        scores(2 * t + 1, sb_scr)
        finish(2 * t, sa_scr)
        scores(2 * t + 2, sa_scr)
        finish(2 * t + 1, sb_scr)
        return carry

    lax.fori_loop(0, n_full // 2, body, 0)
    for d in range(span):
        if d + 1 < span:
            scores(n_full + d + 1, bufs[(d + 1) % 2], q0=(d + 1) * tk)
        finish(n_full + d, bufs[d % 2], diag=d)

    for hd in range(2):
        rows = slice(hd * MLA_V, (hd + 1) * MLA_V)
        acc_scr[rows, :] = acc_scr[rows, :] * (1.0 / l_scr[hd])
    o_ref[0] = jnp.transpose(acc_scr[...]).astype(BF16)


def _mla_attn(qt, k, vt):
    b, s, _ = k.shape
    tq, tk = TQ_MLA, TK_MLA
    pairs = MLA_HEADS // 2
    return pl.pallas_call(
        functools.partial(_mla_attn_kernel, tq=tq, tk=tk),
        grid=(b, pairs, s // tq),
        in_specs=[pl.BlockSpec((1, 2 * HEAD_SLAB, tq), lambda bi, pi, qi: (bi, pi, qi)),
                  pl.BlockSpec((1, s, 2 * HEAD_SLAB), lambda bi, pi, qi: (bi, 0, pi)),
                  pl.BlockSpec((1, 2 * MLA_V, s), lambda bi, pi, qi: (bi, pi, 0))],
        out_specs=pl.BlockSpec((1, tq, 2 * MLA_V), lambda bi, pi, qi: (bi, qi, pi)),
        out_shape=jax.ShapeDtypeStruct((b, s, MLA_HEADS * MLA_V), BF16),
        scratch_shapes=[pltpu.VMEM((2, 1, tq), F32), pltpu.VMEM((2, 1, tq), F32),
                        pltpu.VMEM((2 * MLA_V, tq), F32),
                        pltpu.VMEM((2, tk, tq), F32), pltpu.VMEM((2, tk, tq), F32)],
        compiler_params=_params("arbitrary", "arbitrary", "arbitrary"),
        name="mla_attn",
    )(qt, k, vt)


def _conv_kernel(x_ref, g_ref, win_ref, cw_ref, cb_ref, o_ref, tail_scr, *, tm, dm):
    @pl.when(pl.program_id(1) == 0)
    def _():
        tail_scr[...] = jnp.zeros(tail_scr.shape, F32)

    h = _rms(x_ref[0], g_ref[...]).astype(BF16)
    bcu = _dot(h, win_ref[...])
    gb = bcu[:, :dm]
    u = bcu[:, dm:2 * dm] * bcu[:, 2 * dm:]
    tail = tail_scr[...]
    row = lax.broadcasted_iota(jnp.int32, (tm, 1), 0)
    prev1 = jnp.where(row == 0, tail[7:8], pltpu.roll(u, 1, 0))
    prev2 = jnp.where(row == 0, tail[6:7], jnp.where(row == 1, tail[7:8], pltpu.roll(u, 2, 0)))
    y = cw_ref[0:1] * prev2 + cw_ref[1:2] * prev1 + cw_ref[2:3] * u + cb_ref[...]
    o_ref[0] = (gb * y).astype(BF16)
    tail_scr[...] = u[tm - 8:, :]


def _conv_pre(x, g, win, cw, cb):
    b, s, dm = x.shape
    tm = TM_PROJ
    tok = pl.BlockSpec((1, tm, dm), lambda bi, si: (bi, si, 0))
    return pl.pallas_call(
        functools.partial(_conv_kernel, tm=tm, dm=dm),
        grid=(b, s // tm),
        in_specs=[tok, _resident(g.shape), _resident(win.shape), _resident(cw.shape),
                  _resident(cb.shape)],
        out_specs=tok,
        out_shape=jax.ShapeDtypeStruct((b, s, dm), BF16),
        scratch_shapes=[pltpu.VMEM((8, dm), F32)],
        compiler_params=_params("arbitrary", "arbitrary"),
        name="conv_pre",
    )(x, g, win, cw, cb)


def _sb_proj_kernel(x_ref, g_ref, wk_ref, wqvt_ref, k_ref, qvt_ref, *, dm, scale):
    h = _rms(x_ref[0], g_ref[...]).astype(BF16)
    k_ref[0] = _dot(h, wk_ref[...]).astype(BF16)
    t = _dot_nt(wqvt_ref[...], h)
    qvt_ref[0, :dm] = (t[:dm] * scale).astype(BF16)
    qvt_ref[0, dm:] = t[dm:].astype(BF16)


def _sb_proj(x, g, wk, wqvt, scale):
    b, s, dm = x.shape
    tm = TM_PROJ
    return pl.pallas_call(
        functools.partial(_sb_proj_kernel, dm=dm, scale=scale),
        grid=(b, s // tm),
        in_specs=[pl.BlockSpec((1, tm, dm), lambda bi, si: (bi, si, 0)), _resident(g.shape),
                  _resident(wk.shape), _resident(wqvt.shape)],
        out_specs=[pl.BlockSpec((1, tm, dm), lambda bi, si: (bi, si, 0)),
                   pl.BlockSpec((1, 2 * dm, tm), lambda bi, si: (bi, 0, si))],
        out_shape=[jax.ShapeDtypeStruct((b, s, dm), BF16), jax.ShapeDtypeStruct((b, 2 * dm, s), BF16)],
        compiler_params=_params("arbitrary", "arbitrary"),
        name="sb_proj",
    )(x, g, wk, wqvt)


def _sb_attn_kernel(qt_ref, k_ref, vt_ref, ntri_ref, o_ref, run_scr, acc_scr, za_scr, zb_scr, qh_scr, *,
                    tq, tk):
    span = tq // tk
    assert span % 2 == 0
    qi = pl.program_id(2)
    last = span * (qi + 1) - 1
    hd_w = LANES // 2
    low = lax.broadcasted_iota(jnp.int32, (LANES, 1), 0) < hd_w
    qt = qt_ref[0]
    zero_q = jnp.zeros_like(qt)
    qh_scr[0] = jnp.where(low, qt, zero_q)
    qh_scr[1] = jnp.where(low, zero_q, qt)
    run_scr[...] = jnp.zeros(run_scr.shape, F32)
    acc_scr[...] = jnp.zeros(acc_scr.shape, F32)

    def scores(u, z_scr, q0=0):
        ks = pl.multiple_of(jnp.maximum(last - u, 0) * tk, tk)
        kb = k_ref[0, pl.ds(ks, tk), :]
        for hd in range(2):
            z_scr[hd, :, q0:] = _dot(kb, qh_scr[hd, :, q0:])

    def finish(u, z_scr, diag=None):
        ks = pl.multiple_of((last - u) * tk, tk)
        vtb = vt_ref[0, :, pl.ds(ks, tk)]
        q0 = 0 if diag is None else diag * tk
        cols = slice(q0, tq)

        def masses(hd, triangular):
            z = z_scr[hd, :, cols]
            neg_abs = lax.bitcast_convert_type(
                lax.bitcast_convert_type(z, jnp.uint32) | jnp.uint32(0x80000000), F32)
            softplus = jnp.maximum(z, 0.0) + jnp.log2(1.0 + jnp.exp2(neg_abs))
            keep = None
            mass = softplus
            if triangular:
                keep = (lax.broadcasted_iota(jnp.int32, z.shape, 0)
                        < lax.broadcasted_iota(jnp.int32, z.shape, 1))
                mass = jnp.where(keep, softplus, 0.0)
            ntri = ntri_ref[...]
            sums = _dot(ntri, mass.astype(BF16))
            return z - softplus, keep, sums

        def weights(hd, log_beta, keep, sums):
            rows = slice(hd * hd_w, (hd + 1) * hd_w)
            a = jnp.exp2(log_beta + sums[:tk])
            if keep is not None:
                a = jnp.where(keep, a, 0.0)
            run = run_scr[hd, :, cols]
            acc_scr[rows, cols] += _dot(vtb[rows, :], a.astype(BF16)) * jnp.exp2(run)
            run_scr[hd, :, cols] = run + sums[tk:tk + 1]

        staged = [masses(hd, diag is not None) for hd in range(2)]
        for hd in range(2):
            weights(hd, *staged[hd])

    bufs = (za_scr, zb_scr)
    scores(0, za_scr, q0=(span - 1) * tk)
    for u in range(span):
        scores(u + 1, bufs[(u + 1) % 2], q0=max(span - 2 - u, 0) * tk)
        finish(u, bufs[u % 2], diag=span - 1 - u)

    def body(t, carry):
        u = span + 2 * t
        scores(u + 1, zb_scr)
        finish(u, za_scr)
        scores(u + 2, za_scr)
        finish(u + 1, zb_scr)
        return carry

    lax.fori_loop(0, span * qi // 2, body, 0)
    o_ref[0] = jnp.transpose(acc_scr[...]).astype(BF16)


def _sb_attn(k, qvt, ntri):
    b, s, dm = k.shape
    tq, tk = TQ_SB, TK_SB
    pairs = SB_HEADS // 2
    return pl.pallas_call(
        functools.partial(_sb_attn_kernel, tq=tq, tk=tk),
        grid=(b, pairs, s // tq),
        in_specs=[pl.BlockSpec((1, LANES, tq), lambda bi, pi, qi: (bi, pi, qi)),
                  pl.BlockSpec((1, s, LANES), lambda bi, pi, qi: (bi, 0, pi)),
                  pl.BlockSpec((1, LANES, s), lambda bi, pi, qi: (bi, pairs + pi, 0)),
                  _resident(ntri.shape)],
        out_specs=pl.BlockSpec((1, tq, LANES), lambda bi, pi, qi: (bi, qi, pi)),
        out_shape=jax.ShapeDtypeStruct((b, s, dm), BF16),
        scratch_shapes=[pltpu.VMEM((2, 1, tq), F32), pltpu.VMEM((LANES, tq), F32),
                        pltpu.VMEM((2, tk, tq), F32), pltpu.VMEM((2, tk, tq), F32),
                        pltpu.VMEM((2, LANES, tq), BF16)],
        compiler_params=_params("arbitrary", "arbitrary", "arbitrary"),
        name="sb_attn",
    )(qvt, k, qvt, ntri)


def _mlp_kernel(x_ref, a_ref, wa_ref, g_ref, wup_ref, wdn_ref, gf_ref, o_ref, *, d_ff, final):
    x1 = x_ref[...] + _dot(a_ref[...], wa_ref[...])
    h = _rms(x1, g_ref[...]).astype(BF16)
    acc = x1
    for c in range(d_ff // FF_CHUNK):
        sl = slice(c * FF_CHUNK, (c + 1) * FF_CHUNK)
        r = jnp.maximum(_dot(h, wup_ref[:, sl]), 0.0)
        acc = acc + _dot((r * r).astype(BF16), wdn_ref[sl, :])
    if final:
        acc = _rms(acc, gf_ref[...])
    o_ref[...] = acc


def _mlp(x2, a2, wa, g, wup, wdn, gf, final):
    t, dm = x2.shape
    tm = TM_PROJ
    tok = pl.BlockSpec((tm, dm), lambda i: (i, 0))
    return pl.pallas_call(
        functools.partial(_mlp_kernel, d_ff=wup.shape[1], final=final),
        grid=(t // tm,),
        in_specs=[tok, tok, _resident(wa.shape), _resident(g.shape), _resident(wup.shape),
                  _resident(wdn.shape), _resident(gf.shape)],
        out_specs=tok,
        out_shape=jax.ShapeDtypeStruct((t, dm), F32),
        compiler_params=_params("arbitrary"),
        name="out_proj_mlp",
    )(x2, a2, wa, g, wup, wdn, gf)


def _row(v):
    return v.reshape(1, -1).astype(F32)


def _pad_heads(w, heads, width):
    k = w.shape[0]
    w = w.reshape(k, heads, width)
    return jnp.pad(w, ((0, 0), (0, 0), (0, HEAD_SLAB - width))).reshape(k, heads * HEAD_SLAB)


def _mla_layer(x, posr, invfc, norm_mix, w_dq, norm_q, w_uq, w_dkv, norm_kv, w_uk, w_uv):
    w_kr = jnp.pad(w_dkv[:, MLA_KV_RANK:], ((0, 0), (MLA_NOPE, HEAD_SLAB - MLA_NOPE - MLA_ROPE)))
    wd = jnp.concatenate([w_dq, w_dkv[:, :MLA_KV_RANK], w_kr], axis=1).astype(BF16)
    wuqt = _pad_heads(w_uq, MLA_HEADS, MLA_NOPE + MLA_ROPE).T.astype(BF16)
    wuk = _pad_heads(w_uk, MLA_HEADS, MLA_NOPE).astype(BF16)
    qt, k, vt = _mla_proj(x, posr, _row(norm_mix), wd, _row(norm_q), wuqt, _row(norm_kv), wuk,
                          w_uv.T.astype(BF16), invfc)
    return _mla_attn(qt, k, vt)


def kernel(x, positions, l0_norm_mix, l0_w_dq, l0_norm_q, l0_w_uq, l0_w_dkv, l0_norm_kv, l0_w_uk, l0_w_uv, l0_w_o, l0_norm_mlp, l0_w_up, l0_w_down, l1_norm_mix, l1_w_in, l1_conv_w, l1_conv_b, l1_w_out, l1_norm_mlp, l1_w_up, l1_w_down, l2_norm_mix, l2_w_qkv, l2_w_o, l2_norm_mlp, l2_w_up, l2_w_down, l3_norm_mix, l3_w_dq, l3_norm_q, l3_w_uq, l3_w_dkv, l3_norm_kv, l3_w_uk, l3_w_uv, l3_w_o, l3_norm_mlp, l3_w_up, l3_w_down, final_norm):
    b, s, dm = x.shape
    t = b * s
    posr = positions.reshape(b, 1, s)
    inv_freq = ROPE_THETA ** (-jnp.arange(0, MLA_ROPE, 2, dtype=F32) / MLA_ROPE)
    invfc = inv_freq.reshape(-1, 1)
    tri_row = jnp.arange(TK_SB + BF16_ROWS)[:, None]
    ntri = -((jnp.arange(TK_SB)[None, :] > tri_row) | (tri_row >= TK_SB)).astype(BF16)
    gf = _row(final_norm)

    def mlp(xc, a, wa, g, wup, wdn, final=False):
        out = _mlp(xc.reshape(t, dm), a.reshape(t, dm), wa.astype(BF16), _row(g), wup.astype(BF16),
                   wdn.astype(BF16), gf, final)
        return out.reshape(b, s, dm)

    o = _mla_layer(x, posr, invfc, l0_norm_mix, l0_w_dq, l0_norm_q, l0_w_uq, l0_w_dkv, l0_norm_kv,
                   l0_w_uk, l0_w_uv)
    x = mlp(x, o, l0_w_o, l0_norm_mlp, l0_w_up, l0_w_down)

    gy = _conv_pre(x, _row(l1_norm_mix), l1_w_in.astype(BF16), l1_conv_w.astype(F32), _row(l1_conv_b))
    x = mlp(x, gy, l1_w_out, l1_norm_mlp, l1_w_up, l1_w_down)

    w_qv = jnp.concatenate([l2_w_qkv[:, :dm], l2_w_qkv[:, 2 * dm:]], axis=1)
    k, qvt = _sb_proj(x, _row(l2_norm_mix), l2_w_qkv[:, dm:2 * dm].astype(BF16), w_qv.T.astype(BF16),
                      LOG2_E / math.sqrt(dm // SB_HEADS))
    o = _sb_attn(k, qvt, ntri)
    x = mlp(x, o, l2_w_o, l2_norm_mlp, l2_w_up, l2_w_down)

    o = _mla_layer(x, posr, invfc, l3_norm_mix, l3_w_dq, l3_norm_q, l3_w_uq, l3_w_dkv, l3_norm_kv,
                   l3_w_uk, l3_w_uv)
    return mlp(x, o, l3_w_o, l3_norm_mlp, l3_w_up, l3_w_down, final=True)
```

```python
import functools
import math

import jax
import jax.numpy as jnp
from jax import lax
from jax.experimental import pallas as pl
from jax.experimental.pallas import tpu as pltpu

F32 = jnp.float32
BF16 = jnp.bfloat16

EPS = 1e-6
ROPE_THETA = 10000.0
MLA_HEADS = 16
MLA_NOPE = 64
MLA_ROPE = 32
MLA_V = 64
MLA_Q_RANK = 384
MLA_KV_RANK = 256
SB_HEADS = 16

LANES = 128
BF16_ROWS = 16
HEAD_SLAB = LANES
VMEM_LIMIT = 56 * 1024 * 1024

TM_PROJ = 512
TQ_MLA = 2048
TK_MLA = 512
TQ_SB = 1024
TK_SB = 256
LOG2_E = math.log2(math.e)
FF_CHUNK = 1024


def _params(*sem):
    return pltpu.CompilerParams(dimension_semantics=sem, vmem_limit_bytes=VMEM_LIMIT)


def _resident(shape):
    nd = len(shape)
    return pl.BlockSpec(shape, lambda *_: (0,) * nd, pipeline_mode=pl.Buffered(1))


def _rms(xf, g):
    ms = jnp.mean(xf * xf, axis=-1, keepdims=True)
    return xf * lax.rsqrt(ms + EPS) * g


def _dot(a, b):
    return jnp.dot(a, b, preferred_element_type=F32)


def _dot_nt(a, b):
    return lax.dot_general(a, b, (((1,), (1,)), ((), ())), preferred_element_type=F32)


def _mla_proj_kernel(x_ref, posr_ref, g_ref, wd_ref, nq_ref, wuqt_ref, nkv_ref, wuk_ref, wuvt_ref,
                     invfc_ref, qt_ref, k_ref, vt_ref, *, scale):
    h = _rms(x_ref[0], g_ref[...]).astype(BF16)
    d = _dot(h, wd_ref[...])
    cq = _rms(d[:, :MLA_Q_RANK], nq_ref[...]).astype(BF16)
    ckv = _rms(d[:, MLA_Q_RANK:MLA_Q_RANK + MLA_KV_RANK], nkv_ref[...]).astype(BF16)
    kr = d[:, MLA_Q_RANK + MLA_KV_RANK:]

    half = MLA_ROPE // 2
    ang_t = invfc_ref[...] * posr_ref[0].astype(F32)
    cos_t = jnp.cos(ang_t)
    sin_t = jnp.sin(ang_t)

    kr_t = jnp.transpose(kr)
    k1, k2 = kr_t[MLA_NOPE:MLA_NOPE + half], kr_t[MLA_NOPE + half:MLA_NOPE + MLA_ROPE]
    tm = kr.shape[0]
    kr = jnp.transpose(jnp.concatenate(
        [jnp.zeros((MLA_NOPE, tm), F32), k1 * cos_t - k2 * sin_t, k2 * cos_t + k1 * sin_t,
         jnp.zeros((HEAD_SLAB - MLA_NOPE - MLA_ROPE, tm), F32)], axis=0))
    kf = _dot(ckv, wuk_ref[...])
    nope_lanes = lax.broadcasted_iota(jnp.int32, (1, LANES), 1) < MLA_NOPE
    for pair in range(MLA_HEADS // 2):
        both = kf[:, pair * LANES:(pair + 1) * LANES]
        for odd, lanes in enumerate((both, pltpu.roll(both, LANES - MLA_NOPE, 1))):
            hd = 2 * pair + odd
            k_ref[0, :, hd * HEAD_SLAB:(hd + 1) * HEAD_SLAB] = jnp.where(nope_lanes, lanes, kr).astype(BF16)
    vt_ref[0] = _dot_nt(wuvt_ref[...], ckv).astype(BF16)

    qft = _dot_nt(wuqt_ref[...], cq)
    cos_t = cos_t * scale
    sin_t = sin_t * scale
    hd_rows = MLA_NOPE + MLA_ROPE
    for hd in range(MLA_HEADS):
        s0 = hd * hd_rows
        s1, s2, s3 = s0 + MLA_NOPE, s0 + MLA_NOPE + half, s0 + hd_rows
        r0 = hd * HEAD_SLAB
        r1, r2, r3 = r0 + MLA_NOPE, r0 + MLA_NOPE + half, r0 + hd_rows
        x1, x2 = qft[s1:s2], qft[s2:s3]
        qt_ref[0, r0:r1] = (qft[s0:s1] * scale).astype(BF16)
        qt_ref[0, r1:r2] = (x1 * cos_t - x2 * sin_t).astype(BF16)
        qt_ref[0, r2:r3] = (x2 * cos_t + x1 * sin_t).astype(BF16)
        qt_ref[0, r3:r0 + HEAD_SLAB] = jnp.zeros((HEAD_SLAB - hd_rows, qft.shape[1]), BF16)


def _mla_proj(x, posr, g, wd, nq, wuqt, nkv, wuk, wuvt, invfc):
    b, s, dm = x.shape
    tm = TM_PROJ
    hw = MLA_HEADS * HEAD_SLAB
    vw = MLA_HEADS * MLA_V
    tok = lambda w: pl.BlockSpec((1, tm, w), lambda bi, si: (bi, si, 0))
    tok_t = lambda r: pl.BlockSpec((1, r, tm), lambda bi, si: (bi, 0, si))
    return pl.pallas_call(
        functools.partial(_mla_proj_kernel, scale=LOG2_E / math.sqrt(MLA_NOPE + MLA_ROPE)),
        grid=(b, s // tm),
        in_specs=[tok(dm), tok_t(1), _resident(g.shape), _resident(wd.shape), _resident(nq.shape),
                  _resident(wuqt.shape), _resident(nkv.shape), _resident(wuk.shape),
                  _resident(wuvt.shape), _resident(invfc.shape)],
        out_specs=[tok_t(hw), tok(hw), tok_t(vw)],
        out_shape=[jax.ShapeDtypeStruct((b, hw, s), BF16), jax.ShapeDtypeStruct((b, s, hw), BF16),
                   jax.ShapeDtypeStruct((b, vw, s), BF16)],
        compiler_params=_params("arbitrary", "arbitrary"),
        name="mla_proj",
    )(x, posr, g, wd, nq, wuqt, nkv, wuk, wuvt, invfc)


def _mla_attn_kernel(qt_ref, k_ref, vt_ref, o_ref, m_scr, l_scr, acc_scr, sa_scr, sb_scr, *,
                     tq, tk):
    qi = pl.program_id(2)
    m_scr[...] = jnp.full(m_scr.shape, -jnp.inf, F32)
    l_scr[...] = jnp.zeros(l_scr.shape, F32)
    acc_scr[...] = jnp.zeros(acc_scr.shape, F32)

    def scores(j, s_scr, q0=0):
        ks = pl.multiple_of(j * tk, tk)
        for hd in range(2):
            sl = slice(hd * HEAD_SLAB, (hd + 1) * HEAD_SLAB)
            s_scr[hd, :, q0:] = _dot(k_ref[0, pl.ds(ks, tk), sl], qt_ref[0, sl, q0:])

    def finish(j, s_scr, diag=None):
        ks = pl.multiple_of(j * tk, tk)
        vtb = vt_ref[0, :, pl.ds(ks, tk)]
        chunks = [(0, tq, False)] if diag is None else [(c * tk, tk, c == diag) for c in range(diag, tq // tk)]
        ones = jnp.ones((BF16_ROWS, tk), BF16)
        for hd in range(2):
            rows = slice(hd * MLA_V, (hd + 1) * MLA_V)
            vt_ext = jnp.concatenate([vtb[rows, :], ones], axis=0)
            for q0, nq, triangular in chunks:
                cols = slice(q0, q0 + nq)
                st = s_scr[hd, :, cols]
                if triangular:
                    key = lax.broadcasted_iota(jnp.int32, st.shape, 0)
                    qry = lax.broadcasted_iota(jnp.int32, st.shape, 1)
                    st = jnp.where(key <= qry, st, -jnp.inf)
                m_old = m_scr[hd, :, cols]
                m_new = jnp.maximum(m_old, jnp.max(st, axis=0, keepdims=True))
                alpha = jnp.exp2(m_old - m_new)
                pv = _dot(vt_ext, jnp.exp2(st - m_new).astype(BF16))
                l_scr[hd, :, cols] = alpha * l_scr[hd, :, cols] + pv[MLA_V:MLA_V + 1]
                m_scr[hd, :, cols] = m_new
                acc_scr[rows, cols] = acc_scr[rows, cols] * alpha + pv[:MLA_V]

    span = tq // tk
    assert span % 2 == 0
    n_full = span * qi
    bufs = (sa_scr, sb_scr)

    def own_span(then_full_tiles):
        scores(n_full + span - 1, sa_scr, q0=(span - 1) * tk)
        for i, d in enumerate(reversed(range(span))):
            if d > 0:
                scores(n_full + d - 1, bufs[(i + 1) % 2], q0=(d - 1) * tk)
            elif then_full_tiles:
                scores(0, bufs[(i + 1) % 2])
            finish(n_full + d, bufs[i % 2], diag=d)

    @pl.when(qi == 0)
    def _():
        own_span(False)

    @pl.when(qi > 0)
    def _():
        own_span(True)

        def body(t, carry):
            scores(2 * t + 1, sb_scr)
            finish(2 * t, sa_scr)
            scores(2 * t + 2, sa_scr)
            finish(2 * t + 1, sb_scr)
            return carry

        lax.fori_loop(0, n_full // 2 - 1, body, 0)
        scores(n_full - 1, sb_scr)
        finish(n_full - 2, sa_scr)
        finish(n_full - 1, sb_scr)

    for hd in range(2):
        rows = slice(hd * MLA_V, (hd + 1) * MLA_V)
        acc_scr[rows, :] = acc_scr[rows, :] * (1.0 / l_scr[hd])
    o_ref[0] = jnp.transpose(acc_scr[...]).astype(BF16)


def _mla_attn(qt, k, vt):
    b, s, _ = k.shape
    tq, tk = TQ_MLA, TK_MLA
    assert s % tq == 0 and tq % tk == 0
    pairs = MLA_HEADS // 2
    return pl.pallas_call(
        functools.partial(_mla_attn_kernel, tq=tq, tk=tk),
        grid=(b, pairs, s // tq),
        in_specs=[pl.BlockSpec((1, 2 * HEAD_SLAB, tq), lambda bi, pi, qi: (bi, pi, qi)),
                  pl.BlockSpec((1, s, 2 * HEAD_SLAB), lambda bi, pi, qi: (bi, 0, pi)),
                  pl.BlockSpec((1, 2 * MLA_V, s), lambda bi, pi, qi: (bi, pi, 0))],
        out_specs=pl.BlockSpec((1, tq, 2 * MLA_V), lambda bi, pi, qi: (bi, qi, pi)),
        out_shape=jax.ShapeDtypeStruct((b, s, MLA_HEADS * MLA_V), BF16),
        scratch_shapes=[pltpu.VMEM((2, 1, tq), F32), pltpu.VMEM((2, 1, tq), F32),
                        pltpu.VMEM((2 * MLA_V, tq), F32),
                        pltpu.VMEM((2, tk, tq), F32), pltpu.VMEM((2, tk, tq), F32)],
        compiler_params=_params("arbitrary", "arbitrary", "arbitrary"),
        name="mla_attn",
    )(qt, k, vt)


def _conv_kernel(x_ref, g_ref, win_ref, cw_ref, cb_ref, o_ref, tail_scr, *, tm, dm):
    @pl.when(pl.program_id(1) == 0)
    def _():
        tail_scr[...] = jnp.zeros(tail_scr.shape, F32)

    h = _rms(x_ref[0], g_ref[...]).astype(BF16)
    bcu = _dot(h, win_ref[...])
    gb = bcu[:, :dm]
    u = bcu[:, dm:2 * dm] * bcu[:, 2 * dm:]
    tail = tail_scr[...]
    row = lax.broadcasted_iota(jnp.int32, (tm, 1), 0)
    prev1 = jnp.where(row == 0, tail[7:8], pltpu.roll(u, 1, 0))
    prev2 = jnp.where(row == 0, tail[6:7], jnp.where(row == 1, tail[7:8], pltpu.roll(u, 2, 0)))
    y = cw_ref[0:1] * prev2 + cw_ref[1:2] * prev1 + cw_ref[2:3] * u + cb_ref[...]
    o_ref[0] = (gb * y).astype(BF16)
    tail_scr[...] = u[tm - 8:, :]


def _conv_pre(x, g, win, cw, cb):
    b, s, dm = x.shape
    tm = TM_PROJ
    tok = pl.BlockSpec((1, tm, dm), lambda bi, si: (bi, si, 0))
    return pl.pallas_call(
        functools.partial(_conv_kernel, tm=tm, dm=dm),
        grid=(b, s // tm),
        in_specs=[tok, _resident(g.shape), _resident(win.shape), _resident(cw.shape),
                  _resident(cb.shape)],
        out_specs=tok,
        out_shape=jax.ShapeDtypeStruct((b, s, dm), BF16),
        scratch_shapes=[pltpu.VMEM((8, dm), F32)],
        compiler_params=_params("arbitrary", "arbitrary"),
        name="conv_pre",
    )(x, g, win, cw, cb)


def _sb_proj_kernel(x_ref, g_ref, wk_ref, wqvt_ref, k_ref, qvt_ref, *, dm, scale):
    h = _rms(x_ref[0], g_ref[...]).astype(BF16)
    k_ref[0] = _dot(h, wk_ref[...]).astype(BF16)
    t = _dot_nt(wqvt_ref[...], h)
    qvt_ref[0, :dm] = (t[:dm] * scale).astype(BF16)
    qvt_ref[0, dm:] = t[dm:].astype(BF16)


def _sb_proj(x, g, wk, wqvt, scale):
    b, s, dm = x.shape
    tm = TM_PROJ
    return pl.pallas_call(
        functools.partial(_sb_proj_kernel, dm=dm, scale=scale),
        grid=(b, s // tm),
        in_specs=[pl.BlockSpec((1, tm, dm), lambda bi, si: (bi, si, 0)), _resident(g.shape),
                  _resident(wk.shape), _resident(wqvt.shape)],
        out_specs=[pl.BlockSpec((1, tm, dm), lambda bi, si: (bi, si, 0)),
                   pl.BlockSpec((1, 2 * dm, tm), lambda bi, si: (bi, 0, si))],
        out_shape=[jax.ShapeDtypeStruct((b, s, dm), BF16), jax.ShapeDtypeStruct((b, 2 * dm, s), BF16)],
        compiler_params=_params("arbitrary", "arbitrary"),
        name="sb_proj",
    )(x, g, wk, wqvt)


def _sb_attn_kernel(qt_ref, k_ref, vt_ref, ntri_ref, o_ref, run_scr, acc_scr, za_scr, zb_scr, qh_scr, *,
                    tq, tk):
    span = tq // tk
    assert span % 2 == 0
    qi = pl.program_id(2)
    last = span * (qi + 1) - 1
    hd_w = LANES // 2
    low = lax.broadcasted_iota(jnp.int32, (LANES, 1), 0) < hd_w
    qt = qt_ref[0]
    zero_q = jnp.zeros_like(qt)
    qh_scr[0] = jnp.where(low, qt, zero_q)
    qh_scr[1] = jnp.where(low, zero_q, qt)
    run_scr[...] = jnp.zeros(run_scr.shape, F32)
    acc_scr[...] = jnp.zeros(acc_scr.shape, F32)

    def scores(u, z_scr, q0=0):
        ks = pl.multiple_of(jnp.maximum(last - u, 0) * tk, tk)
        kb = k_ref[0, pl.ds(ks, tk), :]
        for hd in range(2):
            z_scr[hd, :, q0:] = _dot(kb, qh_scr[hd, :, q0:])

    def finish(u, z_scr, diag=None):
        ks = pl.multiple_of((last - u) * tk, tk)
        vtb = vt_ref[0, :, pl.ds(ks, tk)]
        q0 = 0 if diag is None else diag * tk
        cols = slice(q0, tq)

        def masses(hd, triangular):
            z = z_scr[hd, :, cols]
            neg_abs = lax.bitcast_convert_type(
                lax.bitcast_convert_type(z, jnp.uint32) | jnp.uint32(0x80000000), F32)
            softplus = jnp.maximum(z, 0.0) + jnp.log2(1.0 + jnp.exp2(neg_abs))
            keep = None
            mass = softplus
            if triangular:
                keep = (lax.broadcasted_iota(jnp.int32, z.shape, 0)
                        < lax.broadcasted_iota(jnp.int32, z.shape, 1))
                mass = jnp.where(keep, softplus, 0.0)
            ntri = ntri_ref[...]
            sums = _dot(ntri, mass.astype(BF16))
            return z - softplus, keep, sums

        def weights(hd, log_beta, keep, sums):
            rows = slice(hd * hd_w, (hd + 1) * hd_w)
            a = jnp.exp2(log_beta + sums[:tk])
            if keep is not None:
                a = jnp.where(keep, a, 0.0)
            run = run_scr[hd, :, cols]
            acc_scr[rows, cols] += _dot(vtb[rows, :], a.astype(BF16)) * jnp.exp2(run)
            run_scr[hd, :, cols] = run + sums[tk:tk + 1]

        staged = [masses(hd, diag is not None) for hd in range(2)]
        for hd in range(2):
            weights(hd, *staged[hd])

    bufs = (za_scr, zb_scr)
    scores(0, za_scr, q0=(span - 1) * tk)
    for u in range(span):
        scores(u + 1, bufs[(u + 1) % 2], q0=max(span - 2 - u, 0) * tk)
        finish(u, bufs[u % 2], diag=span - 1 - u)

    def body(t, carry):
        u = span + 2 * t
        scores(u + 1, zb_scr)
        finish(u, za_scr)
        scores(u + 2, za_scr)
        finish(u + 1, zb_scr)
        return carry

    lax.fori_loop(0, span * qi // 2, body, 0)
    o_ref[0] = jnp.transpose(acc_scr[...]).astype(BF16)


def _sb_attn(k, qvt, ntri):
    b, s, dm = k.shape
    tq, tk = TQ_SB, TK_SB
    assert s % tq == 0 and tq % tk == 0
    pairs = SB_HEADS // 2
    return pl.pallas_call(
        functools.partial(_sb_attn_kernel, tq=tq, tk=tk),
        grid=(b, pairs, s // tq),
        in_specs=[pl.BlockSpec((1, LANES, tq), lambda bi, pi, qi: (bi, pi, qi)),
                  pl.BlockSpec((1, s, LANES), lambda bi, pi, qi: (bi, 0, pi)),
                  pl.BlockSpec((1, LANES, s), lambda bi, pi, qi: (bi, pairs + pi, 0)),
                  _resident(ntri.shape)],
        out_specs=pl.BlockSpec((1, tq, LANES), lambda bi, pi, qi: (bi, qi, pi)),
        out_shape=jax.ShapeDtypeStruct((b, s, dm), BF16),
        scratch_shapes=[pltpu.VMEM((2, 1, tq), F32), pltpu.VMEM((LANES, tq), F32),
                        pltpu.VMEM((2, tk, tq), F32), pltpu.VMEM((2, tk, tq), F32),
                        pltpu.VMEM((2, LANES, tq), BF16)],
        compiler_params=_params("arbitrary", "arbitrary", "arbitrary"),
        name="sb_attn",
    )(qvt, k, qvt, ntri)


def _mlp_kernel(x_ref, a_ref, wa_ref, g_ref, wup_ref, wdn_ref, gf_ref, o_ref, *, d_ff, final):
    x1 = x_ref[...] + _dot(a_ref[...], wa_ref[...])
    h = _rms(x1, g_ref[...]).astype(BF16)
    acc = x1
    for c in range(d_ff // FF_CHUNK):
        sl = slice(c * FF_CHUNK, (c + 1) * FF_CHUNK)
        r = jnp.maximum(_dot(h, wup_ref[:, sl]), 0.0)
        acc = acc + _dot((r * r).astype(BF16), wdn_ref[sl, :])
    if final:
        acc = _rms(acc, gf_ref[...])
    o_ref[...] = acc


def _mlp(x2, a2, wa, g, wup, wdn, gf, final):
    t, dm = x2.shape
    tm = TM_PROJ
    tok = pl.BlockSpec((tm, dm), lambda i: (i, 0))
    return pl.pallas_call(
        functools.partial(_mlp_kernel, d_ff=wup.shape[1], final=final),
        grid=(t // tm,),
        in_specs=[tok, tok, _resident(wa.shape), _resident(g.shape), _resident(wup.shape),
                  _resident(wdn.shape), _resident(gf.shape)],
        out_specs=tok,
        out_shape=jax.ShapeDtypeStruct((t, dm), F32),
        compiler_params=_params("arbitrary"),
        name="out_proj_mlp",
    )(x2, a2, wa, g, wup, wdn, gf)


def _row(v):
    return v.reshape(1, -1).astype(F32)


def _mla_layer(x, posr, invfc, norm_mix, w_dq, norm_q, w_uq, w_dkv, norm_kv, w_uk, w_uv):
    w_kr = jnp.pad(w_dkv[:, MLA_KV_RANK:], ((0, 0), (MLA_NOPE, HEAD_SLAB - MLA_NOPE - MLA_ROPE)))
    wd = jnp.concatenate([w_dq, w_dkv[:, :MLA_KV_RANK], w_kr], axis=1).astype(BF16)
    wuqt = w_uq.T.astype(BF16)
    wuk = w_uk.astype(BF16)
    qt, k, vt = _mla_proj(x, posr, _row(norm_mix), wd, _row(norm_q), wuqt, _row(norm_kv), wuk,
                          w_uv.T.astype(BF16), invfc)
    return _mla_attn(qt, k, vt)


def kernel(x, positions, l0_norm_mix, l0_w_dq, l0_norm_q, l0_w_uq, l0_w_dkv, l0_norm_kv, l0_w_uk, l0_w_uv, l0_w_o, l0_norm_mlp, l0_w_up, l0_w_down, l1_norm_mix, l1_w_in, l1_conv_w, l1_conv_b, l1_w_out, l1_norm_mlp, l1_w_up, l1_w_down, l2_norm_mix, l2_w_qkv, l2_w_o, l2_norm_mlp, l2_w_up, l2_w_down, l3_norm_mix, l3_w_dq, l3_norm_q, l3_w_uq, l3_w_dkv, l3_norm_kv, l3_w_uk, l3_w_uv, l3_w_o, l3_norm_mlp, l3_w_up, l3_w_down, final_norm):
    b, s, dm = x.shape
    t = b * s
    assert s % TM_PROJ == 0
    posr = positions.reshape(b, 1, s)
    inv_freq = ROPE_THETA ** (-jnp.arange(0, MLA_ROPE, 2, dtype=F32) / MLA_ROPE)
    invfc = inv_freq.reshape(-1, 1)
    tri_row = jnp.arange(TK_SB + BF16_ROWS)[:, None]
    ntri = -((jnp.arange(TK_SB)[None, :] > tri_row) | (tri_row >= TK_SB)).astype(BF16)
    gf = _row(final_norm)

    def mlp(xc, a, wa, g, wup, wdn, final=False):
        out = _mlp(xc.reshape(t, dm), a.reshape(t, dm), wa.astype(BF16), _row(g), wup.astype(BF16),
                   wdn.astype(BF16), gf, final)
        return out.reshape(b, s, dm)

    o = _mla_layer(x, posr, invfc, l0_norm_mix, l0_w_dq, l0_norm_q, l0_w_uq, l0_w_dkv, l0_norm_kv,
                   l0_w_uk, l0_w_uv)
    x = mlp(x, o, l0_w_o, l0_norm_mlp, l0_w_up, l0_w_down)

    gy = _conv_pre(x, _row(l1_norm_mix), l1_w_in.astype(BF16), l1_conv_w.astype(F32), _row(l1_conv_b))
    x = mlp(x, gy, l1_w_out, l1_norm_mlp, l1_w_up, l1_w_down)

    w_qv = jnp.concatenate([l2_w_qkv[:, :dm], l2_w_qkv[:, 2 * dm:]], axis=1)
    k, qvt = _sb_proj(x, _row(l2_norm_mix), l2_w_qkv[:, dm:2 * dm].astype(BF16), w_qv.T.astype(BF16),
                      LOG2_E / math.sqrt(dm // SB_HEADS))
    o = _sb_attn(k, qvt, ntri)
    x = mlp(x, o, l2_w_o, l2_norm_mlp, l2_w_up, l2_w_down)

    o = _mla_layer(x, posr, invfc, l3_norm_mix, l3_w_dq, l3_norm_q, l3_w_uq, l3_w_dkv, l3_norm_kv,
                   l3_w_uk, l3_w_uv)
    return mlp(x, o, l3_w_o, l3_norm_mlp, l3_w_up, l3_w_down, final=True)
```

```python
import functools
import math

import jax
import jax.numpy as jnp
from jax import lax
from jax.experimental import pallas as pl
from jax.experimental.pallas import tpu as pltpu

F32 = jnp.float32
BF16 = jnp.bfloat16

EPS = 1e-6
ROPE_THETA = 10000.0
MLA_HEADS = 16
MLA_NOPE = 64
MLA_ROPE = 32
MLA_V = 64
MLA_Q_RANK = 384
MLA_KV_RANK = 256
SB_HEADS = 16

LANES = 128
BF16_ROWS = 16
HEAD_SLAB = LANES
VMEM_LIMIT = 56 * 1024 * 1024

TM_PROJ = 1024
TM_MLP = 512
TQ_MLA = 2048
TK_MLA = 512
TQ_SB = 1024
TK_SB = 256
LOG2_E = math.log2(math.e)
FF_CHUNK = 1024


def _params(*sem):
    return pltpu.CompilerParams(dimension_semantics=sem, vmem_limit_bytes=VMEM_LIMIT)


def _resident(shape):
    nd = len(shape)
    return pl.BlockSpec(shape, lambda *_: (0,) * nd, pipeline_mode=pl.Buffered(1))


def _rms(xf, g):
    ms = jnp.mean(xf * xf, axis=-1, keepdims=True)
    return xf * lax.rsqrt(ms + EPS) * g


def _dot(a, b):
    return jnp.dot(a, b, preferred_element_type=F32)


def _dot_nt(a, b):
    return lax.dot_general(a, b, (((1,), (1,)), ((), ())), preferred_element_type=F32)


def _mla_proj_kernel(x_ref, posr_ref, g_ref, wd_ref, nq_ref, wuqt_ref, nkv_ref, wuk_ref, wuvt_ref,
                     invfc_ref, qt_ref, k_ref, vt_ref, *, scale):
    h = _rms(x_ref[0], g_ref[...]).astype(BF16)
    d = _dot(h, wd_ref[...])
    cq = _rms(d[:, :MLA_Q_RANK], nq_ref[...]).astype(BF16)
    ckv = _rms(d[:, MLA_Q_RANK:MLA_Q_RANK + MLA_KV_RANK], nkv_ref[...]).astype(BF16)
    kr = d[:, MLA_Q_RANK + MLA_KV_RANK:]

    half = MLA_ROPE // 2
    ang_t = invfc_ref[...] * posr_ref[0].astype(F32)
    cos_t = jnp.cos(ang_t)
    sin_t = jnp.sin(ang_t)

    kr_t = jnp.transpose(kr)
    k1, k2 = kr_t[MLA_NOPE:MLA_NOPE + half], kr_t[MLA_NOPE + half:MLA_NOPE + MLA_ROPE]
    tm = kr.shape[0]
    kr = jnp.transpose(jnp.concatenate(
        [jnp.zeros((MLA_NOPE, tm), F32), k1 * cos_t - k2 * sin_t, k2 * cos_t + k1 * sin_t,
         jnp.zeros((HEAD_SLAB - MLA_NOPE - MLA_ROPE, tm), F32)], axis=0))
    kf = _dot(ckv, wuk_ref[...])
    nope_lanes = lax.broadcasted_iota(jnp.int32, (1, LANES), 1) < MLA_NOPE
    for pair in range(MLA_HEADS // 2):
        both = kf[:, pair * LANES:(pair + 1) * LANES]
        for odd, lanes in enumerate((both, pltpu.roll(both, LANES - MLA_NOPE, 1))):
            hd = 2 * pair + odd
            k_ref[0, :, hd * HEAD_SLAB:(hd + 1) * HEAD_SLAB] = jnp.where(nope_lanes, lanes, kr).astype(BF16)
    vt_ref[0] = _dot_nt(wuvt_ref[...], ckv).astype(BF16)

    qft = _dot_nt(wuqt_ref[...], cq)
    cos_t = cos_t * scale
    sin_t = sin_t * scale
    hd_rows = MLA_NOPE + MLA_ROPE
    for hd in range(MLA_HEADS):
        s0 = hd * hd_rows
        s1, s2, s3 = s0 + MLA_NOPE, s0 + MLA_NOPE + half, s0 + hd_rows
        r0 = hd * HEAD_SLAB
        r1, r2, r3 = r0 + MLA_NOPE, r0 + MLA_NOPE + half, r0 + hd_rows
        x1, x2 = qft[s1:s2], qft[s2:s3]
        qt_ref[0, r0:r1] = (qft[s0:s1] * scale).astype(BF16)
        qt_ref[0, r1:r2] = (x1 * cos_t - x2 * sin_t).astype(BF16)
        qt_ref[0, r2:r3] = (x2 * cos_t + x1 * sin_t).astype(BF16)
        qt_ref[0, r3:r0 + HEAD_SLAB] = jnp.zeros((HEAD_SLAB - hd_rows, qft.shape[1]), BF16)


def _mla_proj(x, posr, g, wd, nq, wuqt, nkv, wuk, wuvt, invfc):
    b, s, dm = x.shape
    tm = TM_PROJ
    hw = MLA_HEADS * HEAD_SLAB
    vw = MLA_HEADS * MLA_V
    tok = lambda w: pl.BlockSpec((1, tm, w), lambda bi, si: (bi, si, 0))
    tok_t = lambda r: pl.BlockSpec((1, r, tm), lambda bi, si: (bi, 0, si))
    return pl.pallas_call(
        functools.partial(_mla_proj_kernel, scale=LOG2_E / math.sqrt(MLA_NOPE + MLA_ROPE)),
        grid=(b, s // tm),
        in_specs=[tok(dm), tok_t(1), _resident(g.shape), _resident(wd.shape), _resident(nq.shape),
                  _resident(wuqt.shape), _resident(nkv.shape), _resident(wuk.shape),
                  _resident(wuvt.shape), _resident(invfc.shape)],
        out_specs=[tok_t(hw), tok(hw), tok_t(vw)],
        out_shape=[jax.ShapeDtypeStruct((b, hw, s), BF16), jax.ShapeDtypeStruct((b, s, hw), BF16),
                   jax.ShapeDtypeStruct((b, vw, s), BF16)],
        compiler_params=_params("arbitrary", "arbitrary"),
        name="mla_proj",
    )(x, posr, g, wd, nq, wuqt, nkv, wuk, wuvt, invfc)


def _mla_attn_kernel(qt_ref, k_ref, vt_ref, o_ref, m_scr, l_scr, acc_scr, sa_scr, sb_scr, *,
                     tq, tk):
    qi = pl.program_id(2)
    m_scr[...] = jnp.full(m_scr.shape, -jnp.inf, F32)
    l_scr[...] = jnp.zeros(l_scr.shape, F32)
    acc_scr[...] = jnp.zeros(acc_scr.shape, F32)

    def scores(j, s_scr, q0=0):
        ks = pl.multiple_of(j * tk, tk)
        for hd in range(2):
            sl = slice(hd * HEAD_SLAB, (hd + 1) * HEAD_SLAB)
            s_scr[hd, :, q0:] = _dot(k_ref[0, pl.ds(ks, tk), sl], qt_ref[0, sl, q0:])

    def finish(j, s_scr, diag=None):
        ks = pl.multiple_of(j * tk, tk)
        vtb = vt_ref[0, :, pl.ds(ks, tk)]
        chunks = [(0, tq, False)] if diag is None else [(c * tk, tk, c == diag) for c in range(diag, tq // tk)]
        ones = jnp.ones((BF16_ROWS, tk), BF16)
        for hd in range(2):
            rows = slice(hd * MLA_V, (hd + 1) * MLA_V)
            vt_ext = jnp.concatenate([vtb[rows, :], ones], axis=0)
            for q0, nq, triangular in chunks:
                cols = slice(q0, q0 + nq)
                st = s_scr[hd, :, cols]
                if triangular:
                    key = lax.broadcasted_iota(jnp.int32, st.shape, 0)
                    qry = lax.broadcasted_iota(jnp.int32, st.shape, 1)
                    st = jnp.where(key <= qry, st, -jnp.inf)
                m_old = m_scr[hd, :, cols]
                m_new = jnp.maximum(m_old, jnp.max(st, axis=0, keepdims=True))
                alpha = jnp.exp2(m_old - m_new)
                pv = _dot(vt_ext, jnp.exp2(st - m_new).astype(BF16))
                l_scr[hd, :, cols] = alpha * l_scr[hd, :, cols] + pv[MLA_V:MLA_V + 1]
                m_scr[hd, :, cols] = m_new
                acc_scr[rows, cols] = acc_scr[rows, cols] * alpha + pv[:MLA_V]

    span = tq // tk
    assert span % 2 == 0
    n_full = span * qi
    bufs = (sa_scr, sb_scr)

    def own_span(then_full_tiles):
        scores(n_full + span - 1, sa_scr, q0=(span - 1) * tk)
        for i, d in enumerate(reversed(range(span))):
            if d > 0:
                scores(n_full + d - 1, bufs[(i + 1) % 2], q0=(d - 1) * tk)
            elif then_full_tiles:
                scores(0, bufs[(i + 1) % 2])
            finish(n_full + d, bufs[i % 2], diag=d)

    @pl.when(qi == 0)
    def _():
        own_span(False)

    @pl.when(qi > 0)
    def _():
        own_span(True)

        def body(t, carry):
            scores(2 * t + 1, sb_scr)
            finish(2 * t, sa_scr)
            scores(2 * t + 2, sa_scr)
            finish(2 * t + 1, sb_scr)
            return carry

        lax.fori_loop(0, n_full // 2 - 1, body, 0)
        scores(n_full - 1, sb_scr)
        finish(n_full - 2, sa_scr)
        finish(n_full - 1, sb_scr)

    for hd in range(2):
        rows = slice(hd * MLA_V, (hd + 1) * MLA_V)
        acc_scr[rows, :] = acc_scr[rows, :] * (1.0 / l_scr[hd])
    o_ref[0] = jnp.transpose(acc_scr[...]).astype(BF16)


def _mla_attn(qt, k, vt):
    b, s, _ = k.shape
    tq, tk = TQ_MLA, TK_MLA
    assert s % tq == 0 and tq % tk == 0
    pairs = MLA_HEADS // 2
    return pl.pallas_call(
        functools.partial(_mla_attn_kernel, tq=tq, tk=tk),
        grid=(b, pairs, s // tq),
        in_specs=[pl.BlockSpec((1, 2 * HEAD_SLAB, tq), lambda bi, pi, qi: (bi, pi, qi)),
                  pl.BlockSpec((1, s, 2 * HEAD_SLAB), lambda bi, pi, qi: (bi, 0, pi)),
                  pl.BlockSpec((1, 2 * MLA_V, s), lambda bi, pi, qi: (bi, pi, 0))],
        out_specs=pl.BlockSpec((1, tq, 2 * MLA_V), lambda bi, pi, qi: (bi, qi, pi)),
        out_shape=jax.ShapeDtypeStruct((b, s, MLA_HEADS * MLA_V), BF16),
        scratch_shapes=[pltpu.VMEM((2, 1, tq), F32), pltpu.VMEM((2, 1, tq), F32),
                        pltpu.VMEM((2 * MLA_V, tq), F32),
                        pltpu.VMEM((2, tk, tq), F32), pltpu.VMEM((2, tk, tq), F32)],
        compiler_params=_params("arbitrary", "arbitrary", "arbitrary"),
        name="mla_attn",
    )(qt, k, vt)


def _conv_kernel(x_ref, g_ref, win_ref, cw_ref, cb_ref, o_ref, tail_scr, *, tm, dm):
    @pl.when(pl.program_id(1) == 0)
    def _():
        tail_scr[...] = jnp.zeros(tail_scr.shape, F32)

    h = _rms(x_ref[0], g_ref[...]).astype(BF16)
    bcu = _dot(h, win_ref[...])
    gb = bcu[:, :dm]
    u = bcu[:, dm:2 * dm] * bcu[:, 2 * dm:]
    tail = tail_scr[...]
    row = lax.broadcasted_iota(jnp.int32, (tm, 1), 0)
    prev1 = jnp.where(row == 0, tail[7:8], pltpu.roll(u, 1, 0))
    prev2 = jnp.where(row == 0, tail[6:7], jnp.where(row == 1, tail[7:8], pltpu.roll(u, 2, 0)))
    y = cw_ref[0:1] * prev2 + cw_ref[1:2] * prev1 + cw_ref[2:3] * u + cb_ref[...]
    o_ref[0] = (gb * y).astype(BF16)
    tail_scr[...] = u[tm - 8:, :]


def _conv_pre(x, g, win, cw, cb):
    b, s, dm = x.shape
    tm = TM_PROJ
    tok = pl.BlockSpec((1, tm, dm), lambda bi, si: (bi, si, 0))
    return pl.pallas_call(
        functools.partial(_conv_kernel, tm=tm, dm=dm),
        grid=(b, s // tm),
        in_specs=[tok, _resident(g.shape), _resident(win.shape), _resident(cw.shape),
                  _resident(cb.shape)],
        out_specs=tok,
        out_shape=jax.ShapeDtypeStruct((b, s, dm), BF16),
        scratch_shapes=[pltpu.VMEM((8, dm), F32)],
        compiler_params=_params("arbitrary", "arbitrary"),
        name="conv_pre",
    )(x, g, win, cw, cb)


def _sb_proj_kernel(x_ref, g_ref, wk_ref, wqvt_ref, k_ref, qvt_ref, *, dm, scale):
    h = _rms(x_ref[0], g_ref[...]).astype(BF16)
    k_ref[0] = _dot(h, wk_ref[...]).astype(BF16)
    t = _dot_nt(wqvt_ref[...], h)
    qvt_ref[0, :dm] = (t[:dm] * scale).astype(BF16)
    qvt_ref[0, dm:] = t[dm:].astype(BF16)


def _sb_proj(x, g, wk, wqvt, scale):
    b, s, dm = x.shape
    tm = TM_PROJ
    return pl.pallas_call(
        functools.partial(_sb_proj_kernel, dm=dm, scale=scale),
        grid=(b, s // tm),
        in_specs=[pl.BlockSpec((1, tm, dm), lambda bi, si: (bi, si, 0)), _resident(g.shape),
                  _resident(wk.shape), _resident(wqvt.shape)],
        out_specs=[pl.BlockSpec((1, tm, dm), lambda bi, si: (bi, si, 0)),
                   pl.BlockSpec((1, 2 * dm, tm), lambda bi, si: (bi, 0, si))],
        out_shape=[jax.ShapeDtypeStruct((b, s, dm), BF16), jax.ShapeDtypeStruct((b, 2 * dm, s), BF16)],
        compiler_params=_params("arbitrary", "arbitrary"),
        name="sb_proj",
    )(x, g, wk, wqvt)


def _sb_attn_kernel(qt_ref, k_ref, vt_ref, ntri_ref, o_ref, run_scr, acc_scr, za_scr, zb_scr, qh_scr, *,
                    tq, tk):
    span = tq // tk
    assert span % 2 == 0
    qi = pl.program_id(2)
    last = span * (qi + 1) - 1
    hd_w = LANES // 2
    low = lax.broadcasted_iota(jnp.int32, (LANES, 1), 0) < hd_w
    qt = qt_ref[0]
    zero_q = jnp.zeros_like(qt)
    qh_scr[0] = jnp.where(low, qt, zero_q)
    qh_scr[1] = jnp.where(low, zero_q, qt)
    run_scr[...] = jnp.zeros(run_scr.shape, F32)
    acc_scr[...] = jnp.zeros(acc_scr.shape, F32)

    def scores(u, z_scr, q0=0):
        ks = pl.multiple_of((last - u) * tk, tk)
        kb = k_ref[0, pl.ds(ks, tk), :]
        for hd in range(2):
            z_scr[hd, :, q0:] = _dot(kb, qh_scr[hd, :, q0:])

    def finish(u, z_scr, diag=None):
        ks = pl.multiple_of((last - u) * tk, tk)
        vtb = vt_ref[0, :, pl.ds(ks, tk)]
        q0 = 0 if diag is None else diag * tk
        cols = slice(q0, tq)

        def masses(hd, triangular):
            z = z_scr[hd, :, cols]
            neg_abs = lax.bitcast_convert_type(
                lax.bitcast_convert_type(z, jnp.uint32) | jnp.uint32(0x80000000), F32)
            softplus = jnp.maximum(z, 0.0) + jnp.log2(1.0 + jnp.exp2(neg_abs))
            keep = None
            mass = softplus
            if triangular:
                keep = (lax.broadcasted_iota(jnp.int32, z.shape, 0)
                        < lax.broadcasted_iota(jnp.int32, z.shape, 1))
                mass = jnp.where(keep, softplus, 0.0)
            ntri = ntri_ref[...]
            sums = _dot(ntri, mass.astype(BF16))
            return z - softplus, keep, sums

        def weights(hd, log_beta, keep, sums):
            rows = slice(hd * hd_w, (hd + 1) * hd_w)
            a = jnp.exp2(log_beta + sums[:tk])
            if keep is not None:
                a = jnp.where(keep, a, 0.0)
            run = run_scr[hd, :, cols]
            acc_scr[rows, cols] += _dot(vtb[rows, :], a.astype(BF16)) * jnp.exp2(run)
            run_scr[hd, :, cols] = run + sums[tk:tk + 1]

        staged = [masses(hd, diag is not None) for hd in range(2)]
        for hd in range(2):
            weights(hd, *staged[hd])

    bufs = (za_scr, zb_scr)
    n_full = span * qi

    def own_span(then_full_tiles):
        scores(0, za_scr, q0=(span - 1) * tk)
        for u in range(span):
            if u + 1 < span:
                scores(u + 1, bufs[(u + 1) % 2], q0=(span - 2 - u) * tk)
            elif then_full_tiles:
                scores(u + 1, bufs[(u + 1) % 2])
            finish(u, bufs[u % 2], diag=span - 1 - u)

    @pl.when(qi == 0)
    def _():
        own_span(False)

    @pl.when(qi > 0)
    def _():
        own_span(True)

        def body(t, carry):
            u = span + 2 * t
            scores(u + 1, zb_scr)
            finish(u, za_scr)
            scores(u + 2, za_scr)
            finish(u + 1, zb_scr)
            return carry

        lax.fori_loop(0, n_full // 2 - 1, body, 0)
        u = span + n_full - 2
        scores(u + 1, zb_scr)
        finish(u, za_scr)
        finish(u + 1, zb_scr)

    o_ref[0] = jnp.transpose(acc_scr[...]).astype(BF16)


def _sb_attn(k, qvt, ntri):
    b, s, dm = k.shape
    tq, tk = TQ_SB, TK_SB
    assert s % tq == 0 and tq % tk == 0
    pairs = SB_HEADS // 2
    return pl.pallas_call(
        functools.partial(_sb_attn_kernel, tq=tq, tk=tk),
        grid=(b, pairs, s // tq),
        in_specs=[pl.BlockSpec((1, LANES, tq), lambda bi, pi, qi: (bi, pi, qi)),
                  pl.BlockSpec((1, s, LANES), lambda bi, pi, qi: (bi, 0, pi)),
                  pl.BlockSpec((1, LANES, s), lambda bi, pi, qi: (bi, pairs + pi, 0)),
                  _resident(ntri.shape)],
        out_specs=pl.BlockSpec((1, tq, LANES), lambda bi, pi, qi: (bi, qi, pi)),
        out_shape=jax.ShapeDtypeStruct((b, s, dm), BF16),
        scratch_shapes=[pltpu.VMEM((2, 1, tq), F32), pltpu.VMEM((LANES, tq), F32),
                        pltpu.VMEM((2, tk, tq), F32), pltpu.VMEM((2, tk, tq), F32),
                        pltpu.VMEM((2, LANES, tq), BF16)],
        compiler_params=_params("arbitrary", "arbitrary", "arbitrary"),
        name="sb_attn",
    )(qvt, k, qvt, ntri)


def _mlp_kernel(x_ref, a_ref, wa_ref, g_ref, wup_ref, wdn_ref, gf_ref, o_ref, *, d_ff, final):
    x1 = x_ref[...] + _dot(a_ref[...], wa_ref[...])
    h = _rms(x1, g_ref[...]).astype(BF16)
    acc = x1
    for c in range(d_ff // FF_CHUNK):
        sl = slice(c * FF_CHUNK, (c + 1) * FF_CHUNK)
        r = jnp.maximum(_dot(h, wup_ref[:, sl]), 0.0)
        acc = acc + _dot((r * r).astype(BF16), wdn_ref[sl, :])
    if final:
        acc = _rms(acc, gf_ref[...])
    o_ref[...] = acc


def _mlp(x2, a2, wa, g, wup, wdn, gf, final):
    t, dm = x2.shape
    tm = TM_MLP
    tok = pl.BlockSpec((tm, dm), lambda i: (i, 0))
    return pl.pallas_call(
        functools.partial(_mlp_kernel, d_ff=wup.shape[1], final=final),
        grid=(t // tm,),
        in_specs=[tok, tok, _resident(wa.shape), _resident(g.shape), _resident(wup.shape),
                  _resident(wdn.shape), _resident(gf.shape)],
        out_specs=tok,
        out_shape=jax.ShapeDtypeStruct((t, dm), F32),
        compiler_params=_params("arbitrary"),
        name="out_proj_mlp",
    )(x2, a2, wa, g, wup, wdn, gf)


def _row(v):
    return v.reshape(1, -1).astype(F32)


def _mla_layer(x, posr, invfc, norm_mix, w_dq, norm_q, w_uq, w_dkv, norm_kv, w_uk, w_uv):
    w_kr = jnp.pad(w_dkv[:, MLA_KV_RANK:], ((0, 0), (MLA_NOPE, HEAD_SLAB - MLA_NOPE - MLA_ROPE)))
    wd = jnp.concatenate([w_dq, w_dkv[:, :MLA_KV_RANK], w_kr], axis=1).astype(BF16)
    wuqt = w_uq.T.astype(BF16)
    wuk = w_uk.astype(BF16)
    qt, k, vt = _mla_proj(x, posr, _row(norm_mix), wd, _row(norm_q), wuqt, _row(norm_kv), wuk,
                          w_uv.T.astype(BF16), invfc)
    return _mla_attn(qt, k, vt)


def kernel(x, positions, l0_norm_mix, l0_w_dq, l0_norm_q, l0_w_uq, l0_w_dkv, l0_norm_kv, l0_w_uk, l0_w_uv, l0_w_o, l0_norm_mlp, l0_w_up, l0_w_down, l1_norm_mix, l1_w_in, l1_conv_w, l1_conv_b, l1_w_out, l1_norm_mlp, l1_w_up, l1_w_down, l2_norm_mix, l2_w_qkv, l2_w_o, l2_norm_mlp, l2_w_up, l2_w_down, l3_norm_mix, l3_w_dq, l3_norm_q, l3_w_uq, l3_w_dkv, l3_norm_kv, l3_w_uk, l3_w_uv, l3_w_o, l3_norm_mlp, l3_w_up, l3_w_down, final_norm):
    b, s, dm = x.shape
    t = b * s
    assert s % TM_PROJ == 0
    posr = positions.reshape(b, 1, s)
    inv_freq = ROPE_THETA ** (-jnp.arange(0, MLA_ROPE, 2, dtype=F32) / MLA_ROPE)
    invfc = inv_freq.reshape(-1, 1)
    tri_row = jnp.arange(TK_SB + BF16_ROWS)[:, None]
    ntri = -((jnp.arange(TK_SB)[None, :] > tri_row) | (tri_row >= TK_SB)).astype(BF16)
    gf = _row(final_norm)

    def mlp(xc, a, wa, g, wup, wdn, final=False):
        out = _mlp(xc.reshape(t, dm), a.reshape(t, dm), wa.astype(BF16), _row(g), wup.astype(BF16),
                   wdn.astype(BF16), gf, final)
        return out.reshape(b, s, dm)

    o = _mla_layer(x, posr, invfc, l0_norm_mix, l0_w_dq, l0_norm_q, l0_w_uq, l0_w_dkv, l0_norm_kv,
                   l0_w_uk, l0_w_uv)
    x = mlp(x, o, l0_w_o, l0_norm_mlp, l0_w_up, l0_w_down)

    gy = _conv_pre(x, _row(l1_norm_mix), l1_w_in.astype(BF16), l1_conv_w.astype(F32), _row(l1_conv_b))
    x = mlp(x, gy, l1_w_out, l1_norm_mlp, l1_w_up, l1_w_down)

    w_qv = jnp.concatenate([l2_w_qkv[:, :dm], l2_w_qkv[:, 2 * dm:]], axis=1)
    k, qvt = _sb_proj(x, _row(l2_norm_mix), l2_w_qkv[:, dm:2 * dm].astype(BF16), w_qv.T.astype(BF16),
                      LOG2_E / math.sqrt(dm // SB_HEADS))
    o = _sb_attn(k, qvt, ntri)
    x = mlp(x, o, l2_w_o, l2_norm_mlp, l2_w_up, l2_w_down)

    o = _mla_layer(x, posr, invfc, l3_norm_mix, l3_w_dq, l3_norm_q, l3_w_uq, l3_w_dkv, l3_norm_kv,
                   l3_w_uk, l3_w_uv)
    return mlp(x, o, l3_w_o, l3_norm_mlp, l3_w_up, l3_w_down, final=True)
```

```python
import functools
import math

import jax
import jax.numpy as jnp
from jax import lax
from jax.experimental import pallas as pl
from jax.experimental.pallas import tpu as pltpu

F32 = jnp.float32
BF16 = jnp.bfloat16

EPS = 1e-6
ROPE_THETA = 10000.0
MLA_HEADS = 16
MLA_NOPE = 64
MLA_ROPE = 32
MLA_V = 64
MLA_Q_RANK = 384
MLA_KV_RANK = 256
SB_HEADS = 16

LANES = 128
BF16_ROWS = 16
HEAD_SLAB = LANES
VMEM_LIMIT = 56 * 1024 * 1024

TM_PROJ = 1024
TM_MLP = 512
TQ_MLA = 2048
TK_MLA = 512
TQ_SB = 1024
TK_SB = 256
LOG2_E = math.log2(math.e)
FF_CHUNK = 1024


def _params(*sem):
    return pltpu.CompilerParams(dimension_semantics=sem, vmem_limit_bytes=VMEM_LIMIT)


def _resident(shape):
    nd = len(shape)
    return pl.BlockSpec(shape, lambda *_: (0,) * nd, pipeline_mode=pl.Buffered(1))


def _rms(xf, g):
    ms = jnp.mean(xf * xf, axis=-1, keepdims=True)
    return xf * lax.rsqrt(ms + EPS) * g


def _dot(a, b):
    return jnp.dot(a, b, preferred_element_type=F32)


def _dot_nt(a, b):
    return lax.dot_general(a, b, (((1,), (1,)), ((), ())), preferred_element_type=F32)


def _mla_proj_kernel(x_ref, posr_ref, g_ref, wd_ref, nq_ref, wuqt_ref, nkv_ref, wuk_ref, wuvt_ref,
                     invfc_ref, qt_ref, k_ref, vt_ref, *, scale):
    h = _rms(x_ref[0], g_ref[...]).astype(BF16)
    d = _dot(h, wd_ref[...])
    cq = _rms(d[:, :MLA_Q_RANK], nq_ref[...]).astype(BF16)
    ckv = _rms(d[:, MLA_Q_RANK:MLA_Q_RANK + MLA_KV_RANK], nkv_ref[...]).astype(BF16)
    kr = d[:, MLA_Q_RANK + MLA_KV_RANK:]

    half = MLA_ROPE // 2
    ang_t = invfc_ref[...] * posr_ref[0].astype(F32)
    cos_t = jnp.cos(ang_t)
    sin_t = jnp.sin(ang_t)

    kr_t = jnp.transpose(kr)
    k1, k2 = kr_t[MLA_NOPE:MLA_NOPE + half], kr_t[MLA_NOPE + half:MLA_NOPE + MLA_ROPE]
    tm = kr.shape[0]
    kr = jnp.transpose(jnp.concatenate(
        [jnp.zeros((MLA_NOPE, tm), F32), k1 * cos_t - k2 * sin_t, k2 * cos_t + k1 * sin_t,
         jnp.zeros((HEAD_SLAB - MLA_NOPE - MLA_ROPE, tm), F32)], axis=0))
    kf = _dot(ckv, wuk_ref[...])
    nope_lanes = lax.broadcasted_iota(jnp.int32, (1, LANES), 1) < MLA_NOPE
    for pair in range(MLA_HEADS // 2):
        both = kf[:, pair * LANES:(pair + 1) * LANES]
        for odd, lanes in enumerate((both, pltpu.roll(both, LANES - MLA_NOPE, 1))):
            hd = 2 * pair + odd
            k_ref[0, :, hd * HEAD_SLAB:(hd + 1) * HEAD_SLAB] = jnp.where(nope_lanes, lanes, kr).astype(BF16)
    vt_ref[0] = _dot_nt(wuvt_ref[...], ckv).astype(BF16)

    qft = _dot_nt(wuqt_ref[...], cq)
    cos_t = cos_t * scale
    sin_t = sin_t * scale
    hd_rows = MLA_NOPE + MLA_ROPE
    for hd in range(MLA_HEADS):
        s0 = hd * hd_rows
        s1, s2, s3 = s0 + MLA_NOPE, s0 + MLA_NOPE + half, s0 + hd_rows
        r0 = hd * HEAD_SLAB
        r1, r2, r3 = r0 + MLA_NOPE, r0 + MLA_NOPE + half, r0 + hd_rows
        x1, x2 = qft[s1:s2], qft[s2:s3]
        qt_ref[0, r0:r1] = (qft[s0:s1] * scale).astype(BF16)
        qt_ref[0, r1:r2] = (x1 * cos_t - x2 * sin_t).astype(BF16)
        qt_ref[0, r2:r3] = (x2 * cos_t + x1 * sin_t).astype(BF16)
        qt_ref[0, r3:r0 + HEAD_SLAB] = jnp.zeros((HEAD_SLAB - hd_rows, qft.shape[1]), BF16)


def _mla_proj(x, posr, g, wd, nq, wuqt, nkv, wuk, wuvt, invfc):
    b, s, dm = x.shape
    tm = TM_PROJ
    hw = MLA_HEADS * HEAD_SLAB
    vw = MLA_HEADS * MLA_V
    tok = lambda w: pl.BlockSpec((1, tm, w), lambda bi, si: (bi, si, 0))
    tok_t = lambda r: pl.BlockSpec((1, r, tm), lambda bi, si: (bi, 0, si))
    return pl.pallas_call(
        functools.partial(_mla_proj_kernel, scale=LOG2_E / math.sqrt(MLA_NOPE + MLA_ROPE)),
        grid=(b, s // tm),
        in_specs=[tok(dm), tok_t(1), _resident(g.shape), _resident(wd.shape), _resident(nq.shape),
                  _resident(wuqt.shape), _resident(nkv.shape), _resident(wuk.shape),
                  _resident(wuvt.shape), _resident(invfc.shape)],
        out_specs=[tok_t(hw), tok(hw), tok_t(vw)],
        out_shape=[jax.ShapeDtypeStruct((b, hw, s), BF16), jax.ShapeDtypeStruct((b, s, hw), BF16),
                   jax.ShapeDtypeStruct((b, vw, s), BF16)],
        compiler_params=_params("arbitrary", "arbitrary"),
        name="mla_proj",
    )(x, posr, g, wd, nq, wuqt, nkv, wuk, wuvt, invfc)


def _mla_attn_kernel(qt_ref, k_ref, vt_ref, o_ref, m_scr, l_scr, acc_scr, sa_scr, sb_scr, *,
                     tq, tk):
    qi = pl.program_id(2)
    m_scr[...] = jnp.full(m_scr.shape, -jnp.inf, F32)
    l_scr[...] = jnp.zeros(l_scr.shape, F32)
    acc_scr[...] = jnp.zeros(acc_scr.shape, F32)

    def scores(j, s_scr, q0=0):
        ks = pl.multiple_of(j * tk, tk)
        for hd in range(2):
            sl = slice(hd * HEAD_SLAB, (hd + 1) * HEAD_SLAB)
            s_scr[hd, :, q0:] = _dot(k_ref[0, pl.ds(ks, tk), sl], qt_ref[0, sl, q0:])

    def finish(j, s_scr, diag=None):
        ks = pl.multiple_of(j * tk, tk)
        vtb = vt_ref[0, :, pl.ds(ks, tk)]
        chunks = [(0, tq, False)] if diag is None else [(c * tk, tk, c == diag) for c in range(diag, tq // tk)]
        ones = jnp.ones((BF16_ROWS, tk), BF16)
        for hd in range(2):
            rows = slice(hd * MLA_V, (hd + 1) * MLA_V)
            vt_ext = jnp.concatenate([vtb[rows, :], ones], axis=0)
            for q0, nq, triangular in chunks:
                cols = slice(q0, q0 + nq)
                st = s_scr[hd, :, cols]
                if triangular:
                    key = lax.broadcasted_iota(jnp.int32, st.shape, 0)
                    qry = lax.broadcasted_iota(jnp.int32, st.shape, 1)
                    st = jnp.where(key <= qry, st, -jnp.inf)
                m_old = m_scr[hd, :, cols]
                m_new = jnp.maximum(m_old, jnp.max(st, axis=0, keepdims=True))
                alpha = jnp.exp2(m_old - m_new)
                pv = _dot(vt_ext, jnp.exp2(st - m_new).astype(BF16))
                l_scr[hd, :, cols] = alpha * l_scr[hd, :, cols] + pv[MLA_V:MLA_V + 1]
                m_scr[hd, :, cols] = m_new
                acc_scr[rows, cols] = acc_scr[rows, cols] * alpha + pv[:MLA_V]

    span = tq // tk
    assert span % 2 == 0
    n_full = span * qi
    bufs = (sa_scr, sb_scr)

    def own_span(then_full_tiles):
        scores(n_full + span - 1, sa_scr, q0=(span - 1) * tk)
        for i, d in enumerate(reversed(range(span))):
            if d > 0:
                scores(n_full + d - 1, bufs[(i + 1) % 2], q0=(d - 1) * tk)
            elif then_full_tiles:
                scores(0, bufs[(i + 1) % 2])
            finish(n_full + d, bufs[i % 2], diag=d)

    @pl.when(qi == 0)
    def _():
        own_span(False)

    @pl.when(qi > 0)
    def _():
        own_span(True)

        def body(t, carry):
            scores(2 * t + 1, sb_scr)
            finish(2 * t, sa_scr)
            scores(2 * t + 2, sa_scr)
            finish(2 * t + 1, sb_scr)
            return carry

        lax.fori_loop(0, n_full // 2 - 1, body, 0)
        scores(n_full - 1, sb_scr)
        finish(n_full - 2, sa_scr)
        finish(n_full - 1, sb_scr)

    for hd in range(2):
        rows = slice(hd * MLA_V, (hd + 1) * MLA_V)
        acc_scr[rows, :] = acc_scr[rows, :] * (1.0 / l_scr[hd])
    o_ref[0] = jnp.transpose(acc_scr[...]).astype(BF16)


def _mla_attn(qt, k, vt):
    b, s, _ = k.shape
    tq, tk = TQ_MLA, TK_MLA
    assert s % tq == 0 and tq % tk == 0
    pairs = MLA_HEADS // 2
    return pl.pallas_call(
        functools.partial(_mla_attn_kernel, tq=tq, tk=tk),
        grid=(b, pairs, s // tq),
        in_specs=[pl.BlockSpec((1, 2 * HEAD_SLAB, tq), lambda bi, pi, qi: (bi, pi, qi)),
                  pl.BlockSpec((1, s, 2 * HEAD_SLAB), lambda bi, pi, qi: (bi, 0, pi)),
                  pl.BlockSpec((1, 2 * MLA_V, s), lambda bi, pi, qi: (bi, pi, 0))],
        out_specs=pl.BlockSpec((1, tq, 2 * MLA_V), lambda bi, pi, qi: (bi, qi, pi)),
        out_shape=jax.ShapeDtypeStruct((b, s, MLA_HEADS * MLA_V), BF16),
        scratch_shapes=[pltpu.VMEM((2, 1, tq), F32), pltpu.VMEM((2, 1, tq), F32),
                        pltpu.VMEM((2 * MLA_V, tq), F32),
                        pltpu.VMEM((2, tk, tq), F32), pltpu.VMEM((2, tk, tq), F32)],
        compiler_params=_params("arbitrary", "arbitrary", "arbitrary"),
        name="mla_attn",
    )(qt, k, vt)


def _conv_kernel(x_ref, g_ref, win_ref, cw_ref, cb_ref, o_ref, tail_scr, *, tm, dm):
    @pl.when(pl.program_id(1) == 0)
    def _():
        tail_scr[...] = jnp.zeros(tail_scr.shape, F32)

    h = _rms(x_ref[0], g_ref[...]).astype(BF16)
    bcu = _dot(h, win_ref[...])
    gb = bcu[:, :dm]
    u = bcu[:, dm:2 * dm] * bcu[:, 2 * dm:]
    tail = tail_scr[...]
    row = lax.broadcasted_iota(jnp.int32, (tm, 1), 0)
    prev1 = jnp.where(row == 0, tail[7:8], pltpu.roll(u, 1, 0))
    prev2 = jnp.where(row == 0, tail[6:7], jnp.where(row == 1, tail[7:8], pltpu.roll(u, 2, 0)))
    y = cw_ref[0:1] * prev2 + cw_ref[1:2] * prev1 + cw_ref[2:3] * u + cb_ref[...]
    o_ref[0] = (gb * y).astype(BF16)
    tail_scr[...] = u[tm - 8:, :]


def _conv_pre(x, g, win, cw, cb):
    b, s, dm = x.shape
    tm = TM_PROJ
    tok = pl.BlockSpec((1, tm, dm), lambda bi, si: (bi, si, 0))
    return pl.pallas_call(
        functools.partial(_conv_kernel, tm=tm, dm=dm),
        grid=(b, s // tm),
        in_specs=[tok, _resident(g.shape), _resident(win.shape), _resident(cw.shape),
                  _resident(cb.shape)],
        out_specs=tok,
        out_shape=jax.ShapeDtypeStruct((b, s, dm), BF16),
        scratch_shapes=[pltpu.VMEM((8, dm), F32)],
        compiler_params=_params("arbitrary", "arbitrary"),
        name="conv_pre",
    )(x, g, win, cw, cb)


def _sb_proj_kernel(x_ref, g_ref, wk_ref, wqvt_ref, k_ref, qvt_ref, *, dm, scale):
    h = _rms(x_ref[0], g_ref[...]).astype(BF16)
    k_ref[0] = _dot(h, wk_ref[...]).astype(BF16)
    t = _dot_nt(wqvt_ref[...], h)
    qvt_ref[0, :dm] = (t[:dm] * scale).astype(BF16)
    qvt_ref[0, dm:] = t[dm:].astype(BF16)


def _sb_proj(x, g, wk, wqvt, scale):
    b, s, dm = x.shape
    tm = TM_PROJ
    return pl.pallas_call(
        functools.partial(_sb_proj_kernel, dm=dm, scale=scale),
        grid=(b, s // tm),
        in_specs=[pl.BlockSpec((1, tm, dm), lambda bi, si: (bi, si, 0)), _resident(g.shape),
                  _resident(wk.shape), _resident(wqvt.shape)],
        out_specs=[pl.BlockSpec((1, tm, dm), lambda bi, si: (bi, si, 0)),
                   pl.BlockSpec((1, 2 * dm, tm), lambda bi, si: (bi, 0, si))],
        out_shape=[jax.ShapeDtypeStruct((b, s, dm), BF16), jax.ShapeDtypeStruct((b, 2 * dm, s), BF16)],
        compiler_params=_params("arbitrary", "arbitrary"),
        name="sb_proj",
    )(x, g, wk, wqvt)


def _sb_attn_kernel(qt_ref, k_ref, vt_ref, ntri_ref, o_ref, run_scr, acc_scr, za_scr, zb_scr, qh_scr, lb_scr,
                    sm_scr, *,
                    tq, tk):
    span = tq // tk
    assert span % 2 == 0
    qi = pl.program_id(2)
    last = span * (qi + 1) - 1
    hd_w = LANES // 2
    low = lax.broadcasted_iota(jnp.int32, (LANES, 1), 0) < hd_w
    qt = qt_ref[0]
    zero_q = jnp.zeros_like(qt)
    qh_scr[0] = jnp.where(low, qt, zero_q)
    qh_scr[1] = jnp.where(low, zero_q, qt)
    run_scr[...] = jnp.zeros(run_scr.shape, F32)
    acc_scr[...] = jnp.zeros(acc_scr.shape, F32)

    def scores(u, z_scr, q0=0):
        ks = pl.multiple_of((last - u) * tk, tk)
        kb = k_ref[0, pl.ds(ks, tk), :]
        for hd in range(2):
            z_scr[hd, :, q0:] = _dot(kb, qh_scr[hd, :, q0:])

    def finish(u, z_scr, diag=None):
        ks = pl.multiple_of((last - u) * tk, tk)
        vtb = vt_ref[0, :, pl.ds(ks, tk)]
        q0 = 0 if diag is None else diag * tk
        cols = slice(q0, tq)

        def masses(hd, triangular):
            z = z_scr[hd, :, cols]
            neg_abs = lax.bitcast_convert_type(
                lax.bitcast_convert_type(z, jnp.uint32) | jnp.uint32(0x80000000), F32)
            softplus = jnp.maximum(z, 0.0) + jnp.log2(1.0 + jnp.exp2(neg_abs))
            keep = None
            mass = softplus
            if triangular:
                keep = (lax.broadcasted_iota(jnp.int32, z.shape, 0)
                        < lax.broadcasted_iota(jnp.int32, z.shape, 1))
                mass = jnp.where(keep, softplus, 0.0)
            ntri = ntri_ref[...]
            sm_scr[hd, :, cols] = _dot(ntri, mass.astype(BF16))
            lb_scr[hd, :, cols] = z - softplus
            return (keep,)

        def weights(hd, keep):
            rows = slice(hd * hd_w, (hd + 1) * hd_w)
            sums = sm_scr[hd, :, cols]
            a = jnp.exp2(lb_scr[hd, :, cols] + sums[:tk])
            if keep is not None:
                a = jnp.where(keep, a, 0.0)
            run = run_scr[hd, :, cols]
            acc_scr[rows, cols] += _dot(vtb[rows, :], a.astype(BF16)) * jnp.exp2(run)
            run_scr[hd, :, cols] = run + sums[tk:tk + 1]

        staged = [masses(hd, diag is not None) for hd in range(2)]
        for hd in range(2):
            weights(hd, *staged[hd])

    bufs = (za_scr, zb_scr)
    n_full = span * qi

    def own_span(then_full_tiles):
        scores(0, za_scr, q0=(span - 1) * tk)
        for u in range(span):
            if u + 1 < span:
                scores(u + 1, bufs[(u + 1) % 2], q0=(span - 2 - u) * tk)
            elif then_full_tiles:
                scores(u + 1, bufs[(u + 1) % 2])
            finish(u, bufs[u % 2], diag=span - 1 - u)

    @pl.when(qi == 0)
    def _():
        own_span(False)

    @pl.when(qi > 0)
    def _():
        own_span(True)

        def body(t, carry):
            u = span + 2 * t
            scores(u + 1, zb_scr)
            finish(u, za_scr)
            scores(u + 2, za_scr)
            finish(u + 1, zb_scr)
            return carry

        lax.fori_loop(0, n_full // 2 - 1, body, 0)
        u = span + n_full - 2
        scores(u + 1, zb_scr)
        finish(u, za_scr)
        finish(u + 1, zb_scr)

    o_ref[0] = jnp.transpose(acc_scr[...]).astype(BF16)


def _sb_attn(k, qvt, ntri):
    b, s, dm = k.shape
    tq, tk = TQ_SB, TK_SB
    assert s % tq == 0 and tq % tk == 0
    pairs = SB_HEADS // 2
    return pl.pallas_call(
        functools.partial(_sb_attn_kernel, tq=tq, tk=tk),
        grid=(b, pairs, s // tq),
        in_specs=[pl.BlockSpec((1, LANES, tq), lambda bi, pi, qi: (bi, pi, qi)),
                  pl.BlockSpec((1, s, LANES), lambda bi, pi, qi: (bi, 0, pi)),
                  pl.BlockSpec((1, LANES, s), lambda bi, pi, qi: (bi, pairs + pi, 0)),
                  _resident(ntri.shape)],
        out_specs=pl.BlockSpec((1, tq, LANES), lambda bi, pi, qi: (bi, qi, pi)),
        out_shape=jax.ShapeDtypeStruct((b, s, dm), BF16),
        scratch_shapes=[pltpu.VMEM((2, 1, tq), F32), pltpu.VMEM((LANES, tq), F32),
                        pltpu.VMEM((2, tk, tq), F32), pltpu.VMEM((2, tk, tq), F32),
                        pltpu.VMEM((2, LANES, tq), BF16), pltpu.VMEM((2, tk, tq), F32),
                        pltpu.VMEM((2, tk + BF16_ROWS, tq), F32)],
        compiler_params=_params("arbitrary", "arbitrary", "arbitrary"),
        name="sb_attn",
    )(qvt, k, qvt, ntri)


def _mlp_kernel(x_ref, a_ref, wa_ref, g_ref, wup_ref, wdn_ref, gf_ref, o_ref, *, d_ff, final):
    x1 = x_ref[...] + _dot(a_ref[...], wa_ref[...])
    h = _rms(x1, g_ref[...]).astype(BF16)
    acc = x1
    for c in range(d_ff // FF_CHUNK):
        sl = slice(c * FF_CHUNK, (c + 1) * FF_CHUNK)
        r = jnp.maximum(_dot(h, wup_ref[:, sl]), 0.0)
        acc = acc + _dot((r * r).astype(BF16), wdn_ref[sl, :])
    if final:
        acc = _rms(acc, gf_ref[...])
    o_ref[...] = acc


def _mlp(x2, a2, wa, g, wup, wdn, gf, final):
    t, dm = x2.shape
    tm = TM_MLP
    tok = pl.BlockSpec((tm, dm), lambda i: (i, 0))
    return pl.pallas_call(
        functools.partial(_mlp_kernel, d_ff=wup.shape[1], final=final),
        grid=(t // tm,),
        in_specs=[tok, tok, _resident(wa.shape), _resident(g.shape), _resident(wup.shape),
                  _resident(wdn.shape), _resident(gf.shape)],
        out_specs=tok,
        out_shape=jax.ShapeDtypeStruct((t, dm), F32),
        compiler_params=_params("arbitrary"),
        name="out_proj_mlp",
    )(x2, a2, wa, g, wup, wdn, gf)


def _row(v):
    return v.reshape(1, -1).astype(F32)


def _mla_layer(x, posr, invfc, norm_mix, w_dq, norm_q, w_uq, w_dkv, norm_kv, w_uk, w_uv):
    w_kr = jnp.pad(w_dkv[:, MLA_KV_RANK:], ((0, 0), (MLA_NOPE, HEAD_SLAB - MLA_NOPE - MLA_ROPE)))
    wd = jnp.concatenate([w_dq, w_dkv[:, :MLA_KV_RANK], w_kr], axis=1).astype(BF16)
    wuqt = w_uq.T.astype(BF16)
    wuk = w_uk.astype(BF16)
    qt, k, vt = _mla_proj(x, posr, _row(norm_mix), wd, _row(norm_q), wuqt, _row(norm_kv), wuk,
                          w_uv.T.astype(BF16), invfc)
    return _mla_attn(qt, k, vt)


def kernel(x, positions, l0_norm_mix, l0_w_dq, l0_norm_q, l0_w_uq, l0_w_dkv, l0_norm_kv, l0_w_uk, l0_w_uv, l0_w_o, l0_norm_mlp, l0_w_up, l0_w_down, l1_norm_mix, l1_w_in, l1_conv_w, l1_conv_b, l1_w_out, l1_norm_mlp, l1_w_up, l1_w_down, l2_norm_mix, l2_w_qkv, l2_w_o, l2_norm_mlp, l2_w_up, l2_w_down, l3_norm_mix, l3_w_dq, l3_norm_q, l3_w_uq, l3_w_dkv, l3_norm_kv, l3_w_uk, l3_w_uv, l3_w_o, l3_norm_mlp, l3_w_up, l3_w_down, final_norm):
    b, s, dm = x.shape
    t = b * s
    assert s % TM_PROJ == 0
    posr = positions.reshape(b, 1, s)
    inv_freq = ROPE_THETA ** (-jnp.arange(0, MLA_ROPE, 2, dtype=F32) / MLA_ROPE)
    invfc = inv_freq.reshape(-1, 1)
    tri_row = jnp.arange(TK_SB + BF16_ROWS)[:, None]
    ntri = -((jnp.arange(TK_SB)[None, :] > tri_row) | (tri_row >= TK_SB)).astype(BF16)
    gf = _row(final_norm)

    def mlp(xc, a, wa, g, wup, wdn, final=False):
        out = _mlp(xc.reshape(t, dm), a.reshape(t, dm), wa.astype(BF16), _row(g), wup.astype(BF16),
                   wdn.astype(BF16), gf, final)
        return out.reshape(b, s, dm)

    o = _mla_layer(x, posr, invfc, l0_norm_mix, l0_w_dq, l0_norm_q, l0_w_uq, l0_w_dkv, l0_norm_kv,
                   l0_w_uk, l0_w_uv)
    x = mlp(x, o, l0_w_o, l0_norm_mlp, l0_w_up, l0_w_down)

    gy = _conv_pre(x, _row(l1_norm_mix), l1_w_in.astype(BF16), l1_conv_w.astype(F32), _row(l1_conv_b))
    x = mlp(x, gy, l1_w_out, l1_norm_mlp, l1_w_up, l1_w_down)

    w_qv = jnp.concatenate([l2_w_qkv[:, :dm], l2_w_qkv[:, 2 * dm:]], axis=1)
    k, qvt = _sb_proj(x, _row(l2_norm_mix), l2_w_qkv[:, dm:2 * dm].astype(BF16), w_qv.T.astype(BF16),
                      LOG2_E / math.sqrt(dm // SB_HEADS))
    o = _sb_attn(k, qvt, ntri)
    x = mlp(x, o, l2_w_o, l2_norm_mlp, l2_w_up, l2_w_down)

    o = _mla_layer(x, posr, invfc, l3_norm_mix, l3_w_dq, l3_norm_q, l3_w_uq, l3_w_dkv, l3_norm_kv,
                   l3_w_uk, l3_w_uv)
    return mlp(x, o, l3_w_o, l3_norm_mlp, l3_w_up, l3_w_down, final=True)
```

```python
import functools
import math

import jax
import jax.numpy as jnp
from jax import lax
from jax.experimental import pallas as pl
from jax.experimental.pallas import tpu as pltpu

F32 = jnp.float32
BF16 = jnp.bfloat16

EPS = 1e-6
ROPE_THETA = 10000.0
MLA_HEADS = 16
MLA_NOPE = 64
MLA_ROPE = 32
MLA_V = 64
MLA_Q_RANK = 384
MLA_KV_RANK = 256
SB_HEADS = 16

LANES = 128
BF16_ROWS = 16
HEAD_SLAB = LANES
VMEM_LIMIT = 56 * 1024 * 1024

TM_PROJ = 1024
TM_MLP = 512
TQ_MLA = 2048
TK_MLA = 512
TQ_SB = 1024
TK_SB = 256
LOG2_E = math.log2(math.e)
FF_CHUNK = 1024


def _params(*sem):
    return pltpu.CompilerParams(dimension_semantics=sem, vmem_limit_bytes=VMEM_LIMIT)


def _resident(shape):
    nd = len(shape)
    return pl.BlockSpec(shape, lambda *_: (0,) * nd, pipeline_mode=pl.Buffered(1))


def _rms(xf, g):
    ms = jnp.mean(xf * xf, axis=-1, keepdims=True)
    return xf * lax.rsqrt(ms + EPS) * g


def _dot(a, b):
    return jnp.dot(a, b, preferred_element_type=F32)


def _dot_nt(a, b):
    return lax.dot_general(a, b, (((1,), (1,)), ((), ())), preferred_element_type=F32)


def _mla_proj_kernel(x_ref, posr_ref, g_ref, wd_ref, nq_ref, wuqt_ref, nkv_ref, wuk_ref, wuvt_ref,
                     invfc_ref, qt_ref, k_ref, vt_ref, *, scale):
    h = _rms(x_ref[0], g_ref[...]).astype(BF16)
    d = _dot(h, wd_ref[...])
    cq = _rms(d[:, :MLA_Q_RANK], nq_ref[...]).astype(BF16)
    ckv = _rms(d[:, MLA_Q_RANK:MLA_Q_RANK + MLA_KV_RANK], nkv_ref[...]).astype(BF16)
    kr = d[:, MLA_Q_RANK + MLA_KV_RANK:]

    half = MLA_ROPE // 2
    ang_t = invfc_ref[...] * posr_ref[0].astype(F32)
    cos_t = jnp.cos(ang_t)
    sin_t = jnp.sin(ang_t)

    kr_t = jnp.transpose(kr)
    k1, k2 = kr_t[MLA_NOPE:MLA_NOPE + half], kr_t[MLA_NOPE + half:MLA_NOPE + MLA_ROPE]
    tm = kr.shape[0]
    kr = jnp.transpose(jnp.concatenate(
        [jnp.zeros((MLA_NOPE, tm), F32), k1 * cos_t - k2 * sin_t, k2 * cos_t + k1 * sin_t,
         jnp.zeros((HEAD_SLAB - MLA_NOPE - MLA_ROPE, tm), F32)], axis=0))
    kf = _dot(ckv, wuk_ref[...])
    nope_lanes = lax.broadcasted_iota(jnp.int32, (1, LANES), 1) < MLA_NOPE
    for pair in range(MLA_HEADS // 2):
        both = kf[:, pair * LANES:(pair + 1) * LANES]
        for odd, lanes in enumerate((both, pltpu.roll(both, LANES - MLA_NOPE, 1))):
            hd = 2 * pair + odd
            k_ref[0, :, hd * HEAD_SLAB:(hd + 1) * HEAD_SLAB] = jnp.where(nope_lanes, lanes, kr).astype(BF16)
    vt_ref[0] = _dot_nt(wuvt_ref[...], ckv).astype(BF16)

    qft = _dot_nt(wuqt_ref[...], cq)
    cos_t = cos_t * scale
    sin_t = sin_t * scale
    hd_rows = MLA_NOPE + MLA_ROPE
    for hd in range(MLA_HEADS):
        s0 = hd * hd_rows
        s1, s2, s3 = s0 + MLA_NOPE, s0 + MLA_NOPE + half, s0 + hd_rows
        r0 = hd * HEAD_SLAB
        r1, r2, r3 = r0 + MLA_NOPE, r0 + MLA_NOPE + half, r0 + hd_rows
        x1, x2 = qft[s1:s2], qft[s2:s3]
        qt_ref[0, r0:r1] = (qft[s0:s1] * scale).astype(BF16)
        qt_ref[0, r1:r2] = (x1 * cos_t - x2 * sin_t).astype(BF16)
        qt_ref[0, r2:r3] = (x2 * cos_t + x1 * sin_t).astype(BF16)
        qt_ref[0, r3:r0 + HEAD_SLAB] = jnp.zeros((HEAD_SLAB - hd_rows, qft.shape[1]), BF16)


def _mla_proj(x, posr, g, wd, nq, wuqt, nkv, wuk, wuvt, invfc):
    b, s, dm = x.shape
    tm = TM_PROJ
    hw = MLA_HEADS * HEAD_SLAB
    vw = MLA_HEADS * MLA_V
    tok = lambda w: pl.BlockSpec((1, tm, w), lambda bi, si: (bi, si, 0))
    tok_t = lambda r: pl.BlockSpec((1, r, tm), lambda bi, si: (bi, 0, si))
    return pl.pallas_call(
        functools.partial(_mla_proj_kernel, scale=LOG2_E / math.sqrt(MLA_NOPE + MLA_ROPE)),
        grid=(b, s // tm),
        in_specs=[tok(dm), tok_t(1), _resident(g.shape), _resident(wd.shape), _resident(nq.shape),
                  _resident(wuqt.shape), _resident(nkv.shape), _resident(wuk.shape),
                  _resident(wuvt.shape), _resident(invfc.shape)],
        out_specs=[tok_t(hw), tok(hw), tok_t(vw)],
        out_shape=[jax.ShapeDtypeStruct((b, hw, s), BF16), jax.ShapeDtypeStruct((b, s, hw), BF16),
                   jax.ShapeDtypeStruct((b, vw, s), BF16)],
        compiler_params=_params("arbitrary", "arbitrary"),
        name="mla_proj",
    )(x, posr, g, wd, nq, wuqt, nkv, wuk, wuvt, invfc)


def _mla_attn_kernel(qt_ref, k_ref, vt_ref, o_ref, m_scr, l_scr, acc_scr, sa_scr, sb_scr, *,
                     tq, tk):
    qi = pl.program_id(2)
    m_scr[...] = jnp.full(m_scr.shape, -jnp.inf, F32)
    l_scr[...] = jnp.zeros(l_scr.shape, F32)
    acc_scr[...] = jnp.zeros(acc_scr.shape, F32)

    half = tk // 2

    def scores(j, s_scr, q0=0, diagonal=False):
        ks = pl.multiple_of(j * tk, tk)
        for hd in range(2):
            sl = slice(hd * HEAD_SLAB, (hd + 1) * HEAD_SLAB)
            if diagonal:
                s_scr[hd, :half, q0:q0 + half] = _dot(k_ref[0, pl.ds(ks, half), sl], qt_ref[0, sl, q0:q0 + half])
                s_scr[hd, :, q0 + half:] = _dot(k_ref[0, pl.ds(ks, tk), sl], qt_ref[0, sl, q0 + half:])
            else:
                s_scr[hd, :, q0:] = _dot(k_ref[0, pl.ds(ks, tk), sl], qt_ref[0, sl, q0:])

    def finish(j, s_scr, diag=None):
        ks = pl.multiple_of(j * tk, tk)
        vtb = vt_ref[0, :, pl.ds(ks, tk)]
        if diag is None:
            chunks = [(0, tq, tk, None)]
        else:
            d0 = diag * tk
            chunks = [(d0, half, half, 0), (d0 + half, half, tk, half)]
            chunks += [(c * tk, tk, tk, None) for c in range(diag + 1, tq // tk)]
        ones = jnp.ones((BF16_ROWS, tk), BF16)
        for hd in range(2):
            rows = slice(hd * MLA_V, (hd + 1) * MLA_V)
            vt_ext = jnp.concatenate([vtb[rows, :], ones], axis=0)
            for q0, nq, nk, offset in chunks:
                cols = slice(q0, q0 + nq)
                st = s_scr[hd, :nk, cols]
                if offset is not None:
                    key = lax.broadcasted_iota(jnp.int32, st.shape, 0)
                    qry = lax.broadcasted_iota(jnp.int32, st.shape, 1) + offset
                    st = jnp.where(key <= qry, st, -jnp.inf)
                m_old = m_scr[hd, :, cols]
                m_new = jnp.maximum(m_old, jnp.max(st, axis=0, keepdims=True))
                alpha = jnp.exp2(m_old - m_new)
                pv = _dot(vt_ext[:, :nk], jnp.exp2(st - m_new).astype(BF16))
                l_scr[hd, :, cols] = alpha * l_scr[hd, :, cols] + pv[MLA_V:MLA_V + 1]
                m_scr[hd, :, cols] = m_new
                acc_scr[rows, cols] = acc_scr[rows, cols] * alpha + pv[:MLA_V]

    span = tq // tk
    assert span % 2 == 0
    n_full = span * qi
    bufs = (sa_scr, sb_scr)

    def own_span(then_full_tiles):
        scores(n_full + span - 1, sa_scr, q0=(span - 1) * tk, diagonal=True)
        for i, d in enumerate(reversed(range(span))):
            if d > 0:
                scores(n_full + d - 1, bufs[(i + 1) % 2], q0=(d - 1) * tk, diagonal=True)
            elif then_full_tiles:
                scores(0, bufs[(i + 1) % 2])
            finish(n_full + d, bufs[i % 2], diag=d)

    @pl.when(qi == 0)
    def _():
        own_span(False)

    @pl.when(qi > 0)
    def _():
        own_span(True)

        def body(t, carry):
            scores(2 * t + 1, sb_scr)
            finish(2 * t, sa_scr)
            scores(2 * t + 2, sa_scr)
            finish(2 * t + 1, sb_scr)
            return carry

        lax.fori_loop(0, n_full // 2 - 1, body, 0)
        scores(n_full - 1, sb_scr)
        finish(n_full - 2, sa_scr)
        finish(n_full - 1, sb_scr)

    for hd in range(2):
        rows = slice(hd * MLA_V, (hd + 1) * MLA_V)
        acc_scr[rows, :] = acc_scr[rows, :] * (1.0 / l_scr[hd])
    o_ref[0] = jnp.transpose(acc_scr[...]).astype(BF16)


def _mla_attn(qt, k, vt):
    b, s, _ = k.shape
    tq, tk = TQ_MLA, TK_MLA
    assert s % tq == 0 and tq % tk == 0
    pairs = MLA_HEADS // 2
    return pl.pallas_call(
        functools.partial(_mla_attn_kernel, tq=tq, tk=tk),
        grid=(b, pairs, s // tq),
        in_specs=[pl.BlockSpec((1, 2 * HEAD_SLAB, tq), lambda bi, pi, qi: (bi, pi, qi)),
                  pl.BlockSpec((1, s, 2 * HEAD_SLAB), lambda bi, pi, qi: (bi, 0, pi)),
                  pl.BlockSpec((1, 2 * MLA_V, s), lambda bi, pi, qi: (bi, pi, 0))],
        out_specs=pl.BlockSpec((1, tq, 2 * MLA_V), lambda bi, pi, qi: (bi, qi, pi)),
        out_shape=jax.ShapeDtypeStruct((b, s, MLA_HEADS * MLA_V), BF16),
        scratch_shapes=[pltpu.VMEM((2, 1, tq), F32), pltpu.VMEM((2, 1, tq), F32),
                        pltpu.VMEM((2 * MLA_V, tq), F32),
                        pltpu.VMEM((2, tk, tq), F32), pltpu.VMEM((2, tk, tq), F32)],
        compiler_params=_params("arbitrary", "arbitrary", "arbitrary"),
        name="mla_attn",
    )(qt, k, vt)


def _conv_kernel(x_ref, g_ref, win_ref, cw_ref, cb_ref, o_ref, tail_scr, *, tm, dm):
    @pl.when(pl.program_id(1) == 0)
    def _():
        tail_scr[...] = jnp.zeros(tail_scr.shape, F32)

    h = _rms(x_ref[0], g_ref[...]).astype(BF16)
    bcu = _dot(h, win_ref[...])
    gb = bcu[:, :dm]
    u = bcu[:, dm:2 * dm] * bcu[:, 2 * dm:]
    tail = tail_scr[...]
    row = lax.broadcasted_iota(jnp.int32, (tm, 1), 0)
    prev1 = jnp.where(row == 0, tail[7:8], pltpu.roll(u, 1, 0))
    prev2 = jnp.where(row == 0, tail[6:7], jnp.where(row == 1, tail[7:8], pltpu.roll(u, 2, 0)))
    y = cw_ref[0:1] * prev2 + cw_ref[1:2] * prev1 + cw_ref[2:3] * u + cb_ref[...]
    o_ref[0] = (gb * y).astype(BF16)
    tail_scr[...] = u[tm - 8:, :]


def _conv_pre(x, g, win, cw, cb):
    b, s, dm = x.shape
    tm = TM_PROJ
    tok = pl.BlockSpec((1, tm, dm), lambda bi, si: (bi, si, 0))
    return pl.pallas_call(
        functools.partial(_conv_kernel, tm=tm, dm=dm),
        grid=(b, s // tm),
        in_specs=[tok, _resident(g.shape), _resident(win.shape), _resident(cw.shape),
                  _resident(cb.shape)],
        out_specs=tok,
        out_shape=jax.ShapeDtypeStruct((b, s, dm), BF16),
        scratch_shapes=[pltpu.VMEM((8, dm), F32)],
        compiler_params=_params("arbitrary", "arbitrary"),
        name="conv_pre",
    )(x, g, win, cw, cb)


def _sb_proj_kernel(x_ref, g_ref, wk_ref, wqvt_ref, k_ref, qvt_ref, *, dm, scale):
    h = _rms(x_ref[0], g_ref[...]).astype(BF16)
    k_ref[0] = _dot(h, wk_ref[...]).astype(BF16)
    t = _dot_nt(wqvt_ref[...], h)
    qvt_ref[0, :dm] = (t[:dm] * scale).astype(BF16)
    qvt_ref[0, dm:] = t[dm:].astype(BF16)


def _sb_proj(x, g, wk, wqvt, scale):
    b, s, dm = x.shape
    tm = TM_PROJ
    return pl.pallas_call(
        functools.partial(_sb_proj_kernel, dm=dm, scale=scale),
        grid=(b, s // tm),
        in_specs=[pl.BlockSpec((1, tm, dm), lambda bi, si: (bi, si, 0)), _resident(g.shape),
                  _resident(wk.shape), _resident(wqvt.shape)],
        out_specs=[pl.BlockSpec((1, tm, dm), lambda bi, si: (bi, si, 0)),
                   pl.BlockSpec((1, 2 * dm, tm), lambda bi, si: (bi, 0, si))],
        out_shape=[jax.ShapeDtypeStruct((b, s, dm), BF16), jax.ShapeDtypeStruct((b, 2 * dm, s), BF16)],
        compiler_params=_params("arbitrary", "arbitrary"),
        name="sb_proj",
    )(x, g, wk, wqvt)


def _sb_attn_kernel(qt_ref, k_ref, vt_ref, ntri_ref, o_ref, run_scr, acc_scr, za_scr, zb_scr, qh_scr, lb_scr,
                    sm_scr, *,
                    tq, tk):
    span = tq // tk
    assert span % 2 == 0
    qi = pl.program_id(2)
    last = span * (qi + 1) - 1
    hd_w = LANES // 2
    low = lax.broadcasted_iota(jnp.int32, (LANES, 1), 0) < hd_w
    qt = qt_ref[0]
    zero_q = jnp.zeros_like(qt)
    qh_scr[0] = jnp.where(low, qt, zero_q)
    qh_scr[1] = jnp.where(low, zero_q, qt)
    run_scr[...] = jnp.zeros(run_scr.shape, F32)
    acc_scr[...] = jnp.zeros(acc_scr.shape, F32)

    def scores(u, z_scr, q0=0):
        ks = pl.multiple_of((last - u) * tk, tk)
        kb = k_ref[0, pl.ds(ks, tk), :]
        for hd in range(2):
            z_scr[hd, :, q0:] = _dot(kb, qh_scr[hd, :, q0:])

    def finish(u, z_scr, diag=None):
        ks = pl.multiple_of((last - u) * tk, tk)
        vtb = vt_ref[0, :, pl.ds(ks, tk)]
        q0 = 0 if diag is None else diag * tk
        cols = slice(q0, tq)

        def masses(hd, triangular):
            z = z_scr[hd, :, cols]
            neg_abs = lax.bitcast_convert_type(
                lax.bitcast_convert_type(z, jnp.uint32) | jnp.uint32(0x80000000), F32)
            softplus = jnp.maximum(z, 0.0) + jnp.log2(1.0 + jnp.exp2(neg_abs))
            keep = None
            mass = softplus
            if triangular:
                keep = (lax.broadcasted_iota(jnp.int32, z.shape, 0)
                        < lax.broadcasted_iota(jnp.int32, z.shape, 1))
                mass = jnp.where(keep, softplus, 0.0)
            ntri = ntri_ref[...]
            sm_scr[hd, :, cols] = _dot(ntri, mass.astype(BF16))
            lb_scr[hd, :, cols] = z - softplus
            return (keep,)

        def weights(hd, keep):
            rows = slice(hd * hd_w, (hd + 1) * hd_w)
            sums = sm_scr[hd, :, cols]
            a = jnp.exp2(lb_scr[hd, :, cols] + sums[:tk])
            if keep is not None:
                a = jnp.where(keep, a, 0.0)
            run = run_scr[hd, :, cols]
            acc_scr[rows, cols] += _dot(vtb[rows, :], a.astype(BF16)) * jnp.exp2(run)
            run_scr[hd, :, cols] = run + sums[tk:tk + 1]

        staged = [masses(hd, diag is not None) for hd in range(2)]
        for hd in range(2):
            weights(hd, *staged[hd])

    bufs = (za_scr, zb_scr)
    n_full = span * qi

    def own_span(then_full_tiles):
        scores(0, za_scr, q0=(span - 1) * tk)
        for u in range(span):
            if u + 1 < span:
                scores(u + 1, bufs[(u + 1) % 2], q0=(span - 2 - u) * tk)
            elif then_full_tiles:
                scores(u + 1, bufs[(u + 1) % 2])
            finish(u, bufs[u % 2], diag=span - 1 - u)

    @pl.when(qi == 0)
    def _():
        own_span(False)

    @pl.when(qi > 0)
    def _():
        own_span(True)

        def body(t, carry):
            u = span + 2 * t
            scores(u + 1, zb_scr)
            finish(u, za_scr)
            scores(u + 2, za_scr)
            finish(u + 1, zb_scr)
            return carry

        lax.fori_loop(0, n_full // 2 - 1, body, 0)
        u = span + n_full - 2
        scores(u + 1, zb_scr)
        finish(u, za_scr)
        finish(u + 1, zb_scr)

    o_ref[0] = jnp.transpose(acc_scr[...]).astype(BF16)


def _sb_attn(k, qvt, ntri):
    b, s, dm = k.shape
    tq, tk = TQ_SB, TK_SB
    assert s % tq == 0 and tq % tk == 0
    pairs = SB_HEADS // 2
    return pl.pallas_call(
        functools.partial(_sb_attn_kernel, tq=tq, tk=tk),
        grid=(b, pairs, s // tq),
        in_specs=[pl.BlockSpec((1, LANES, tq), lambda bi, pi, qi: (bi, pi, qi)),
                  pl.BlockSpec((1, s, LANES), lambda bi, pi, qi: (bi, 0, pi)),
                  pl.BlockSpec((1, LANES, s), lambda bi, pi, qi: (bi, pairs + pi, 0)),
                  _resident(ntri.shape)],
        out_specs=pl.BlockSpec((1, tq, LANES), lambda bi, pi, qi: (bi, qi, pi)),
        out_shape=jax.ShapeDtypeStruct((b, s, dm), BF16),
        scratch_shapes=[pltpu.VMEM((2, 1, tq), F32), pltpu.VMEM((LANES, tq), F32),
                        pltpu.VMEM((2, tk, tq), F32), pltpu.VMEM((2, tk, tq), F32),
                        pltpu.VMEM((2, LANES, tq), BF16), pltpu.VMEM((2, tk, tq), F32),
                        pltpu.VMEM((2, tk + BF16_ROWS, tq), F32)],
        compiler_params=_params("arbitrary", "arbitrary", "arbitrary"),
        name="sb_attn",
    )(qvt, k, qvt, ntri)


def _mlp_kernel(x_ref, a_ref, wa_ref, g_ref, wup_ref, wdn_ref, gf_ref, o_ref, *, d_ff, final):
    x1 = x_ref[...] + _dot(a_ref[...], wa_ref[...])
    h = _rms(x1, g_ref[...]).astype(BF16)
    acc = x1
    for c in range(d_ff // FF_CHUNK):
        sl = slice(c * FF_CHUNK, (c + 1) * FF_CHUNK)
        r = jnp.maximum(_dot(h, wup_ref[:, sl]), 0.0)
        acc = acc + _dot((r * r).astype(BF16), wdn_ref[sl, :])
    if final:
        acc = _rms(acc, gf_ref[...])
    o_ref[...] = acc


def _mlp(x2, a2, wa, g, wup, wdn, gf, final):
    t, dm = x2.shape
    tm = TM_MLP
    tok = pl.BlockSpec((tm, dm), lambda i: (i, 0))
    return pl.pallas_call(
        functools.partial(_mlp_kernel, d_ff=wup.shape[1], final=final),
        grid=(t // tm,),
        in_specs=[tok, tok, _resident(wa.shape), _resident(g.shape), _resident(wup.shape),
                  _resident(wdn.shape), _resident(gf.shape)],
        out_specs=tok,
        out_shape=jax.ShapeDtypeStruct((t, dm), F32),
        compiler_params=_params("arbitrary"),
        name="out_proj_mlp",
    )(x2, a2, wa, g, wup, wdn, gf)


def _row(v):
    return v.reshape(1, -1).astype(F32)


def _mla_layer(x, posr, invfc, norm_mix, w_dq, norm_q, w_uq, w_dkv, norm_kv, w_uk, w_uv):
    w_kr = jnp.pad(w_dkv[:, MLA_KV_RANK:], ((0, 0), (MLA_NOPE, HEAD_SLAB - MLA_NOPE - MLA_ROPE)))
    wd = jnp.concatenate([w_dq, w_dkv[:, :MLA_KV_RANK], w_kr], axis=1).astype(BF16)
    wuqt = w_uq.T.astype(BF16)
    wuk = w_uk.astype(BF16)
    qt, k, vt = _mla_proj(x, posr, _row(norm_mix), wd, _row(norm_q), wuqt, _row(norm_kv), wuk,
                          w_uv.T.astype(BF16), invfc)
    return _mla_attn(qt, k, vt)


def kernel(x, positions, l0_norm_mix, l0_w_dq, l0_norm_q, l0_w_uq, l0_w_dkv, l0_norm_kv, l0_w_uk, l0_w_uv, l0_w_o, l0_norm_mlp, l0_w_up, l0_w_down, l1_norm_mix, l1_w_in, l1_conv_w, l1_conv_b, l1_w_out, l1_norm_mlp, l1_w_up, l1_w_down, l2_norm_mix, l2_w_qkv, l2_w_o, l2_norm_mlp, l2_w_up, l2_w_down, l3_norm_mix, l3_w_dq, l3_norm_q, l3_w_uq, l3_w_dkv, l3_norm_kv, l3_w_uk, l3_w_uv, l3_w_o, l3_norm_mlp, l3_w_up, l3_w_down, final_norm):
    b, s, dm = x.shape
    t = b * s
    assert s % TM_PROJ == 0
    posr = positions.reshape(b, 1, s)
    inv_freq = ROPE_THETA ** (-jnp.arange(0, MLA_ROPE, 2, dtype=F32) / MLA_ROPE)
    invfc = inv_freq.reshape(-1, 1)
    tri_row = jnp.arange(TK_SB + BF16_ROWS)[:, None]
    ntri = -((jnp.arange(TK_SB)[None, :] > tri_row) | (tri_row >= TK_SB)).astype(BF16)
    gf = _row(final_norm)

    def mlp(xc, a, wa, g, wup, wdn, final=False):
        out = _mlp(xc.reshape(t, dm), a.reshape(t, dm), wa.astype(BF16), _row(g), wup.astype(BF16),
                   wdn.astype(BF16), gf, final)
        return out.reshape(b, s, dm)

    o = _mla_layer(x, posr, invfc, l0_norm_mix, l0_w_dq, l0_norm_q, l0_w_uq, l0_w_dkv, l0_norm_kv,
                   l0_w_uk, l0_w_uv)
    x = mlp(x, o, l0_w_o, l0_norm_mlp, l0_w_up, l0_w_down)

    gy = _conv_pre(x, _row(l1_norm_mix), l1_w_in.astype(BF16), l1_conv_w.astype(F32), _row(l1_conv_b))
    x = mlp(x, gy, l1_w_out, l1_norm_mlp, l1_w_up, l1_w_down)

    w_qv = jnp.concatenate([l2_w_qkv[:, :dm], l2_w_qkv[:, 2 * dm:]], axis=1)
    k, qvt = _sb_proj(x, _row(l2_norm_mix), l2_w_qkv[:, dm:2 * dm].astype(BF16), w_qv.T.astype(BF16),
                      LOG2_E / math.sqrt(dm // SB_HEADS))
    o = _sb_attn(k, qvt, ntri)
    x = mlp(x, o, l2_w_o, l2_norm_mlp, l2_w_up, l2_w_down)

    o = _mla_layer(x, posr, invfc, l3_norm_mix, l3_w_dq, l3_norm_q, l3_w_uq, l3_w_dkv, l3_norm_kv,
                   l3_w_uk, l3_w_uv)
    return mlp(x, o, l3_w_o, l3_norm_mlp, l3_w_up, l3_w_down, final=True)
```

```python
import functools
import math

import jax
import jax.numpy as jnp
from jax import lax
from jax.experimental import pallas as pl
from jax.experimental.pallas import tpu as pltpu

F32 = jnp.float32
BF16 = jnp.bfloat16

EPS = 1e-6
ROPE_THETA = 10000.0
MLA_HEADS = 16
MLA_NOPE = 64
MLA_ROPE = 32
MLA_V = 64
MLA_Q_RANK = 384
MLA_KV_RANK = 256
SB_HEADS = 16

LANES = 128
BF16_ROWS = 16
HEAD_SLAB = LANES
VMEM_LIMIT = 56 * 1024 * 1024

TM_PROJ = 1024
TM_MLP = 512
TQ_MLA = 2048
TK_MLA = 512
TQ_SB = 1024
TK_SB = 256
LOG2_E = math.log2(math.e)
FF_CHUNK = 1024


def _params(*sem):
    return pltpu.CompilerParams(dimension_semantics=sem, vmem_limit_bytes=VMEM_LIMIT)


def _resident(shape):
    nd = len(shape)
    return pl.BlockSpec(shape, lambda *_: (0,) * nd, pipeline_mode=pl.Buffered(1))


def _rms(xf, g):
    ms = jnp.mean(xf * xf, axis=-1, keepdims=True)
    return xf * lax.rsqrt(ms + EPS) * g


def _dot(a, b):
    return jnp.dot(a, b, preferred_element_type=F32)


def _dot_nt(a, b):
    return lax.dot_general(a, b, (((1,), (1,)), ((), ())), preferred_element_type=F32)


def _mla_proj_kernel(x_ref, posr_ref, g_ref, wd_ref, nq_ref, wuqt_ref, nkv_ref, wuk_ref, wuvt_ref,
                     invfc_ref, qt_ref, k_ref, vt_ref, *, scale):
    h = _rms(x_ref[0], g_ref[...]).astype(BF16)
    d = _dot(h, wd_ref[...])
    cq = _rms(d[:, :MLA_Q_RANK], nq_ref[...]).astype(BF16)
    ckv = _rms(d[:, MLA_Q_RANK:MLA_Q_RANK + MLA_KV_RANK], nkv_ref[...]).astype(BF16)
    kr = d[:, MLA_Q_RANK + MLA_KV_RANK:]

    half = MLA_ROPE // 2
    ang_t = invfc_ref[...] * posr_ref[0].astype(F32)
    cos_t = jnp.cos(ang_t)
    sin_t = jnp.sin(ang_t)

    kr_t = jnp.transpose(kr)
    k1, k2 = kr_t[MLA_NOPE:MLA_NOPE + half], kr_t[MLA_NOPE + half:MLA_NOPE + MLA_ROPE]
    tm = kr.shape[0]
    kr = jnp.transpose(jnp.concatenate(
        [jnp.zeros((MLA_NOPE, tm), F32), k1 * cos_t - k2 * sin_t, k2 * cos_t + k1 * sin_t,
         jnp.zeros((HEAD_SLAB - MLA_NOPE - MLA_ROPE, tm), F32)], axis=0))
    kf = _dot(ckv, wuk_ref[...])
    nope_lanes = lax.broadcasted_iota(jnp.int32, (1, LANES), 1) < MLA_NOPE
    for pair in range(MLA_HEADS // 2):
        both = kf[:, pair * LANES:(pair + 1) * LANES]
        for odd, lanes in enumerate((both, pltpu.roll(both, LANES - MLA_NOPE, 1))):
            hd = 2 * pair + odd
            k_ref[0, :, hd * HEAD_SLAB:(hd + 1) * HEAD_SLAB] = jnp.where(nope_lanes, lanes, kr).astype(BF16)
    vt_ref[0] = _dot_nt(wuvt_ref[...], ckv).astype(BF16)

    qft = _dot_nt(wuqt_ref[...], cq)
    cos_t = cos_t * scale
    sin_t = sin_t * scale
    hd_rows = MLA_NOPE + MLA_ROPE
    for hd in range(MLA_HEADS):
        s0 = hd * hd_rows
        s1, s2, s3 = s0 + MLA_NOPE, s0 + MLA_NOPE + half, s0 + hd_rows
        r0 = hd * HEAD_SLAB
        r1, r2, r3 = r0 + MLA_NOPE, r0 + MLA_NOPE + half, r0 + hd_rows
        x1, x2 = qft[s1:s2], qft[s2:s3]
        qt_ref[0, r0:r1] = (qft[s0:s1] * scale).astype(BF16)
        qt_ref[0, r1:r2] = (x1 * cos_t - x2 * sin_t).astype(BF16)
        qt_ref[0, r2:r3] = (x2 * cos_t + x1 * sin_t).astype(BF16)
        qt_ref[0, r3:r0 + HEAD_SLAB] = jnp.zeros((HEAD_SLAB - hd_rows, qft.shape[1]), BF16)


def _mla_proj(x, posr, g, wd, nq, wuqt, nkv, wuk, wuvt, invfc):
    b, s, dm = x.shape
    tm = TM_PROJ
    hw = MLA_HEADS * HEAD_SLAB
    vw = MLA_HEADS * MLA_V
    tok = lambda w: pl.BlockSpec((1, tm, w), lambda bi, si: (bi, si, 0))
    tok_t = lambda r: pl.BlockSpec((1, r, tm), lambda bi, si: (bi, 0, si))
    return pl.pallas_call(
        functools.partial(_mla_proj_kernel, scale=LOG2_E / math.sqrt(MLA_NOPE + MLA_ROPE)),
        grid=(b, s // tm),
        in_specs=[tok(dm), tok_t(1), _resident(g.shape), _resident(wd.shape), _resident(nq.shape),
                  _resident(wuqt.shape), _resident(nkv.shape), _resident(wuk.shape),
                  _resident(wuvt.shape), _resident(invfc.shape)],
        out_specs=[tok_t(hw), tok(hw), tok_t(vw)],
        out_shape=[jax.ShapeDtypeStruct((b, hw, s), BF16), jax.ShapeDtypeStruct((b, s, hw), BF16),
                   jax.ShapeDtypeStruct((b, vw, s), BF16)],
        compiler_params=_params("arbitrary", "arbitrary"),
        name="mla_proj",
    )(x, posr, g, wd, nq, wuqt, nkv, wuk, wuvt, invfc)


def _mla_attn_kernel(qt_ref, k_ref, vt_ref, o_ref, m_scr, l_scr, acc_scr, sa_scr, sb_scr, *,
                     tq, tk):
    qi = pl.program_id(2)
    m_scr[...] = jnp.full(m_scr.shape, -jnp.inf, F32)
    l_scr[...] = jnp.zeros(l_scr.shape, F32)
    acc_scr[...] = jnp.zeros(acc_scr.shape, F32)

    half = tk // 2

    def scores(j, s_scr, q0=0, diagonal=False):
        ks = pl.multiple_of(j * tk, tk)
        for hd in range(2):
            sl = slice(hd * HEAD_SLAB, (hd + 1) * HEAD_SLAB)
            if diagonal:
                s_scr[hd, :half, q0:q0 + half] = _dot(k_ref[0, pl.ds(ks, half), sl], qt_ref[0, sl, q0:q0 + half])
                s_scr[hd, :, q0 + half:] = _dot(k_ref[0, pl.ds(ks, tk), sl], qt_ref[0, sl, q0 + half:])
            else:
                s_scr[hd, :, q0:] = _dot(k_ref[0, pl.ds(ks, tk), sl], qt_ref[0, sl, q0:])

    def finish(j, s_scr, diag=None):
        ks = pl.multiple_of(j * tk, tk)
        vtb = vt_ref[0, :, pl.ds(ks, tk)]
        if diag is None:
            chunks = [(0, tq, tk, None)]
        else:
            d0 = diag * tk
            chunks = [(d0, half, half, 0), (d0 + half, half, tk, half)]
            chunks += [(c * tk, tk, tk, None) for c in range(diag + 1, tq // tk)]
        ones = jnp.ones((BF16_ROWS, tk), BF16)
        for hd in range(2):
            rows = slice(hd * MLA_V, (hd + 1) * MLA_V)
            vt_ext = jnp.concatenate([vtb[rows, :], ones], axis=0)
            for q0, nq, nk, offset in chunks:
                cols = slice(q0, q0 + nq)
                st = s_scr[hd, :nk, cols]
                if offset is not None:
                    key = lax.broadcasted_iota(jnp.int32, st.shape, 0)
                    qry = lax.broadcasted_iota(jnp.int32, st.shape, 1) + offset
                    st = jnp.where(key <= qry, st, -jnp.inf)
                m_old = m_scr[hd, :, cols]
                m_new = jnp.maximum(m_old, jnp.max(st, axis=0, keepdims=True))
                alpha = jnp.exp2(m_old - m_new)
                pv = _dot(vt_ext[:, :nk], jnp.exp2(st - m_new).astype(BF16))
                l_scr[hd, :, cols] = alpha * l_scr[hd, :, cols] + pv[MLA_V:MLA_V + 1]
                m_scr[hd, :, cols] = m_new
                acc_scr[rows, cols] = acc_scr[rows, cols] * alpha + pv[:MLA_V]

    span = tq // tk
    assert span % 2 == 0
    n_full = span * qi
    bufs = (sa_scr, sb_scr)

    def own_span(then_full_tiles):
        scores(n_full + span - 1, sa_scr, q0=(span - 1) * tk, diagonal=True)
        for i, d in enumerate(reversed(range(span))):
            if d > 0:
                scores(n_full + d - 1, bufs[(i + 1) % 2], q0=(d - 1) * tk, diagonal=True)
            elif then_full_tiles:
                scores(0, bufs[(i + 1) % 2])
            finish(n_full + d, bufs[i % 2], diag=d)

    @pl.when(qi == 0)
    def _():
        own_span(False)

    @pl.when(qi > 0)
    def _():
        own_span(True)

        def body(t, carry):
            scores(2 * t + 1, sb_scr)
            finish(2 * t, sa_scr)
            scores(2 * t + 2, sa_scr)
            finish(2 * t + 1, sb_scr)
            return carry

        lax.fori_loop(0, n_full // 2 - 1, body, 0)
        scores(n_full - 1, sb_scr)
        finish(n_full - 2, sa_scr)
        finish(n_full - 1, sb_scr)

    for hd in range(2):
        rows = slice(hd * MLA_V, (hd + 1) * MLA_V)
        acc_scr[rows, :] = acc_scr[rows, :] * (1.0 / l_scr[hd])
    o_ref[0] = jnp.transpose(acc_scr[...]).astype(BF16)


def _mla_attn(qt, k, vt):
    b, s, _ = k.shape
    tq, tk = TQ_MLA, TK_MLA
    assert s % tq == 0 and tq % tk == 0
    pairs = MLA_HEADS // 2
    return pl.pallas_call(
        functools.partial(_mla_attn_kernel, tq=tq, tk=tk),
        grid=(b, pairs, s // tq),
        in_specs=[pl.BlockSpec((1, 2 * HEAD_SLAB, tq), lambda bi, pi, qi: (bi, pi, qi)),
                  pl.BlockSpec((1, s, 2 * HEAD_SLAB), lambda bi, pi, qi: (bi, 0, pi)),
                  pl.BlockSpec((1, 2 * MLA_V, s), lambda bi, pi, qi: (bi, pi, 0))],
        out_specs=pl.BlockSpec((1, tq, 2 * MLA_V), lambda bi, pi, qi: (bi, qi, pi)),
        out_shape=jax.ShapeDtypeStruct((b, s, MLA_HEADS * MLA_V), BF16),
        scratch_shapes=[pltpu.VMEM((2, 1, tq), F32), pltpu.VMEM((2, 1, tq), F32),
                        pltpu.VMEM((2 * MLA_V, tq), F32),
                        pltpu.VMEM((2, tk, tq), F32), pltpu.VMEM((2, tk, tq), F32)],
        compiler_params=_params("arbitrary", "arbitrary", "arbitrary"),
        name="mla_attn",
    )(qt, k, vt)


def _conv_kernel(x_ref, g_ref, win_ref, cw_ref, cb_ref, o_ref, tail_scr, *, tm, dm):
    @pl.when(pl.program_id(1) == 0)
    def _():
        tail_scr[...] = jnp.zeros(tail_scr.shape, F32)

    h = _rms(x_ref[0], g_ref[...]).astype(BF16)
    bcu = _dot(h, win_ref[...])
    gb = bcu[:, :dm]
    u = bcu[:, dm:2 * dm] * bcu[:, 2 * dm:]
    tail = tail_scr[...]
    row = lax.broadcasted_iota(jnp.int32, (tm, 1), 0)
    prev1 = jnp.where(row == 0, tail[7:8], pltpu.roll(u, 1, 0))
    prev2 = jnp.where(row == 0, tail[6:7], jnp.where(row == 1, tail[7:8], pltpu.roll(u, 2, 0)))
    y = cw_ref[0:1] * prev2 + cw_ref[1:2] * prev1 + cw_ref[2:3] * u + cb_ref[...]
    o_ref[0] = (gb * y).astype(BF16)
    tail_scr[...] = u[tm - 8:, :]


def _conv_pre(x, g, win, cw, cb):
    b, s, dm = x.shape
    tm = TM_PROJ
    tok = pl.BlockSpec((1, tm, dm), lambda bi, si: (bi, si, 0))
    return pl.pallas_call(
        functools.partial(_conv_kernel, tm=tm, dm=dm),
        grid=(b, s // tm),
        in_specs=[tok, _resident(g.shape), _resident(win.shape), _resident(cw.shape),
                  _resident(cb.shape)],
        out_specs=tok,
        out_shape=jax.ShapeDtypeStruct((b, s, dm), BF16),
        scratch_shapes=[pltpu.VMEM((8, dm), F32)],
        compiler_params=_params("arbitrary", "arbitrary"),
        name="conv_pre",
    )(x, g, win, cw, cb)


def _sb_proj_kernel(x_ref, g_ref, wk_ref, wqvt_ref, k_ref, qvt_ref, *, dm, scale):
    h = _rms(x_ref[0], g_ref[...]).astype(BF16)
    k_ref[0] = _dot(h, wk_ref[...]).astype(BF16)
    t = _dot_nt(wqvt_ref[...], h)
    qvt_ref[0, :dm] = (t[:dm] * scale).astype(BF16)
    qvt_ref[0, dm:] = t[dm:].astype(BF16)


def _sb_proj(x, g, wk, wqvt, scale):
    b, s, dm = x.shape
    tm = TM_PROJ
    return pl.pallas_call(
        functools.partial(_sb_proj_kernel, dm=dm, scale=scale),
        grid=(b, s // tm),
        in_specs=[pl.BlockSpec((1, tm, dm), lambda bi, si: (bi, si, 0)), _resident(g.shape),
                  _resident(wk.shape), _resident(wqvt.shape)],
        out_specs=[pl.BlockSpec((1, tm, dm), lambda bi, si: (bi, si, 0)),
                   pl.BlockSpec((1, 2 * dm, tm), lambda bi, si: (bi, 0, si))],
        out_shape=[jax.ShapeDtypeStruct((b, s, dm), BF16), jax.ShapeDtypeStruct((b, 2 * dm, s), BF16)],
        compiler_params=_params("arbitrary", "arbitrary"),
        name="sb_proj",
    )(x, g, wk, wqvt)


def _sb_attn_kernel(qt_ref, k_ref, vt_ref, ntri_ref, o_ref, run_scr, acc_scr, za_scr, zb_scr, qh_scr, lb_scr,
                    sm_scr, *,
                    tq, tk):
    span = tq // tk
    assert span % 2 == 0
    qi = pl.program_id(2)
    last = span * (qi + 1) - 1
    hd_w = LANES // 2
    low = lax.broadcasted_iota(jnp.int32, (LANES, 1), 0) < hd_w
    qt = qt_ref[0]
    zero_q = jnp.zeros_like(qt)
    qh_scr[0] = jnp.where(low, qt, zero_q)
    qh_scr[1] = jnp.where(low, zero_q, qt)
    run_scr[...] = jnp.zeros(run_scr.shape, F32)
    acc_scr[...] = jnp.zeros(acc_scr.shape, F32)

    def scores(u, z_scr, q0=0):
        ks = pl.multiple_of((last - u) * tk, tk)
        kb = k_ref[0, pl.ds(ks, tk), :]
        for hd in range(2):
            z_scr[hd, :, q0:] = _dot(kb, qh_scr[hd, :, q0:])

    def finish(u, z_scr, diag=None):
        ks = pl.multiple_of((last - u) * tk, tk)
        vtb = vt_ref[0, :, pl.ds(ks, tk)]
        q0 = 0 if diag is None else diag * tk
        cols = slice(q0, tq)

        def masses(hd, triangular):
            z = z_scr[hd, :, cols]
            neg_abs = lax.bitcast_convert_type(
                lax.bitcast_convert_type(z.astype(BF16), jnp.uint16) | jnp.uint16(0x8000), BF16)
            softplus = jnp.maximum(z, 0.0) + jnp.log2(1.0 + jnp.exp2(neg_abs).astype(F32))
            keep = None
            mass = softplus
            if triangular:
                keep = (lax.broadcasted_iota(jnp.int32, z.shape, 0)
                        < lax.broadcasted_iota(jnp.int32, z.shape, 1))
                mass = jnp.where(keep, softplus, 0.0)
            ntri = ntri_ref[...]
            sm_scr[hd, :, cols] = _dot(ntri, mass.astype(BF16))
            lb_scr[hd, :, cols] = z - softplus
            return (keep,)

        def weights(hd, keep):
            rows = slice(hd * hd_w, (hd + 1) * hd_w)
            sums = sm_scr[hd, :, cols]
            a = jnp.exp2(lb_scr[hd, :, cols] + sums[:tk])
            if keep is not None:
                a = jnp.where(keep, a, 0.0)
            run = run_scr[hd, :, cols]
            acc_scr[rows, cols] += _dot(vtb[rows, :], a.astype(BF16)) * jnp.exp2(run)
            run_scr[hd, :, cols] = run + sums[tk:tk + 1]

        staged = [masses(hd, diag is not None) for hd in range(2)]
        for hd in range(2):
            weights(hd, *staged[hd])

    bufs = (za_scr, zb_scr)
    n_full = span * qi

    def own_span(then_full_tiles):
        scores(0, za_scr, q0=(span - 1) * tk)
        for u in range(span):
            if u + 1 < span:
                scores(u + 1, bufs[(u + 1) % 2], q0=(span - 2 - u) * tk)
            elif then_full_tiles:
                scores(u + 1, bufs[(u + 1) % 2])
            finish(u, bufs[u % 2], diag=span - 1 - u)

    @pl.when(qi == 0)
    def _():
        own_span(False)

    @pl.when(qi > 0)
    def _():
        own_span(True)

        def body(t, carry):
            u = span + 2 * t
            scores(u + 1, zb_scr)
            finish(u, za_scr)
            scores(u + 2, za_scr)
            finish(u + 1, zb_scr)
            return carry

        lax.fori_loop(0, n_full // 2 - 1, body, 0)
        u = span + n_full - 2
        scores(u + 1, zb_scr)
        finish(u, za_scr)
        finish(u + 1, zb_scr)

    o_ref[0] = jnp.transpose(acc_scr[...]).astype(BF16)


def _sb_attn(k, qvt, ntri):
    b, s, dm = k.shape
    tq, tk = TQ_SB, TK_SB
    assert s % tq == 0 and tq % tk == 0
    pairs = SB_HEADS // 2
    return pl.pallas_call(
        functools.partial(_sb_attn_kernel, tq=tq, tk=tk),
        grid=(b, pairs, s // tq),
        in_specs=[pl.BlockSpec((1, LANES, tq), lambda bi, pi, qi: (bi, pi, qi)),
                  pl.BlockSpec((1, s, LANES), lambda bi, pi, qi: (bi, 0, pi)),
                  pl.BlockSpec((1, LANES, s), lambda bi, pi, qi: (bi, pairs + pi, 0)),
                  _resident(ntri.shape)],
        out_specs=pl.BlockSpec((1, tq, LANES), lambda bi, pi, qi: (bi, qi, pi)),
        out_shape=jax.ShapeDtypeStruct((b, s, dm), BF16),
        scratch_shapes=[pltpu.VMEM((2, 1, tq), F32), pltpu.VMEM((LANES, tq), F32),
                        pltpu.VMEM((2, tk, tq), F32), pltpu.VMEM((2, tk, tq), F32),
                        pltpu.VMEM((2, LANES, tq), BF16), pltpu.VMEM((2, tk, tq), F32),
                        pltpu.VMEM((2, tk + BF16_ROWS, tq), F32)],
        compiler_params=_params("arbitrary", "arbitrary", "arbitrary"),
        name="sb_attn",
    )(qvt, k, qvt, ntri)


def _mlp_kernel(x_ref, a_ref, wa_ref, g_ref, wup_ref, wdn_ref, gf_ref, o_ref, *, d_ff, final):
    x1 = x_ref[...] + _dot(a_ref[...], wa_ref[...])
    h = _rms(x1, g_ref[...]).astype(BF16)
    acc = x1
    for c in range(d_ff // FF_CHUNK):
        sl = slice(c * FF_CHUNK, (c + 1) * FF_CHUNK)
        r = jnp.maximum(_dot(h, wup_ref[:, sl]), 0.0)
        acc = acc + _dot((r * r).astype(BF16), wdn_ref[sl, :])
    if final:
        acc = _rms(acc, gf_ref[...])
    o_ref[...] = acc


def _mlp(x2, a2, wa, g, wup, wdn, gf, final):
    t, dm = x2.shape
    tm = TM_MLP
    tok = pl.BlockSpec((tm, dm), lambda i: (i, 0))
    return pl.pallas_call(
        functools.partial(_mlp_kernel, d_ff=wup.shape[1], final=final),
        grid=(t // tm,),
        in_specs=[tok, tok, _resident(wa.shape), _resident(g.shape), _resident(wup.shape),
                  _resident(wdn.shape), _resident(gf.shape)],
        out_specs=tok,
        out_shape=jax.ShapeDtypeStruct((t, dm), F32),
        compiler_params=_params("arbitrary"),
        name="out_proj_mlp",
    )(x2, a2, wa, g, wup, wdn, gf)


def _row(v):
    return v.reshape(1, -1).astype(F32)


def _mla_layer(x, posr, invfc, norm_mix, w_dq, norm_q, w_uq, w_dkv, norm_kv, w_uk, w_uv):
    w_kr = jnp.pad(w_dkv[:, MLA_KV_RANK:], ((0, 0), (MLA_NOPE, HEAD_SLAB - MLA_NOPE - MLA_ROPE)))
    wd = jnp.concatenate([w_dq, w_dkv[:, :MLA_KV_RANK], w_kr], axis=1).astype(BF16)
    wuqt = w_uq.T.astype(BF16)
    wuk = w_uk.astype(BF16)
    qt, k, vt = _mla_proj(x, posr, _row(norm_mix), wd, _row(norm_q), wuqt, _row(norm_kv), wuk,
                          w_uv.T.astype(BF16), invfc)
    return _mla_attn(qt, k, vt)


def kernel(x, positions, l0_norm_mix, l0_w_dq, l0_norm_q, l0_w_uq, l0_w_dkv, l0_norm_kv, l0_w_uk, l0_w_uv, l0_w_o, l0_norm_mlp, l0_w_up, l0_w_down, l1_norm_mix, l1_w_in, l1_conv_w, l1_conv_b, l1_w_out, l1_norm_mlp, l1_w_up, l1_w_down, l2_norm_mix, l2_w_qkv, l2_w_o, l2_norm_mlp, l2_w_up, l2_w_down, l3_norm_mix, l3_w_dq, l3_norm_q, l3_w_uq, l3_w_dkv, l3_norm_kv, l3_w_uk, l3_w_uv, l3_w_o, l3_norm_mlp, l3_w_up, l3_w_down, final_norm):
    b, s, dm = x.shape
    t = b * s
    assert s % TM_PROJ == 0
    posr = positions.reshape(b, 1, s)
    inv_freq = ROPE_THETA ** (-jnp.arange(0, MLA_ROPE, 2, dtype=F32) / MLA_ROPE)
    invfc = inv_freq.reshape(-1, 1)
    tri_row = jnp.arange(TK_SB + BF16_ROWS)[:, None]
    ntri = -((jnp.arange(TK_SB)[None, :] > tri_row) | (tri_row >= TK_SB)).astype(BF16)
    gf = _row(final_norm)

    def mlp(xc, a, wa, g, wup, wdn, final=False):
        out = _mlp(xc.reshape(t, dm), a.reshape(t, dm), wa.astype(BF16), _row(g), wup.astype(BF16),
                   wdn.astype(BF16), gf, final)
        return out.reshape(b, s, dm)

    o = _mla_layer(x, posr, invfc, l0_norm_mix, l0_w_dq, l0_norm_q, l0_w_uq, l0_w_dkv, l0_norm_kv,
                   l0_w_uk, l0_w_uv)
    x = mlp(x, o, l0_w_o, l0_norm_mlp, l0_w_up, l0_w_down)

    gy = _conv_pre(x, _row(l1_norm_mix), l1_w_in.astype(BF16), l1_conv_w.astype(F32), _row(l1_conv_b))
    x = mlp(x, gy, l1_w_out, l1_norm_mlp, l1_w_up, l1_w_down)

    w_qv = jnp.concatenate([l2_w_qkv[:, :dm], l2_w_qkv[:, 2 * dm:]], axis=1)
    k, qvt = _sb_proj(x, _row(l2_norm_mix), l2_w_qkv[:, dm:2 * dm].astype(BF16), w_qv.T.astype(BF16),
                      LOG2_E / math.sqrt(dm // SB_HEADS))
    o = _sb_attn(k, qvt, ntri)
    x = mlp(x, o, l2_w_o, l2_norm_mlp, l2_w_up, l2_w_down)

    o = _mla_layer(x, posr, invfc, l3_norm_mix, l3_w_dq, l3_norm_q, l3_w_uq, l3_w_dkv, l3_norm_kv,
                   l3_w_uk, l3_w_uv)
    return mlp(x, o, l3_w_o, l3_norm_mlp, l3_w_up, l3_w_down, final=True)
```

```python
import functools
import math

import jax
import jax.numpy as jnp
from jax import lax
from jax.experimental import pallas as pl
from jax.experimental.pallas import tpu as pltpu

F32 = jnp.float32
BF16 = jnp.bfloat16

EPS = 1e-6
ROPE_THETA = 10000.0
MLA_HEADS = 16
MLA_NOPE = 64
MLA_ROPE = 32
MLA_V = 64
MLA_Q_RANK = 384
MLA_KV_RANK = 256
SB_HEADS = 16

LANES = 128
BF16_ROWS = 16
HEAD_SLAB = LANES
VMEM_LIMIT = 56 * 1024 * 1024

TM_PROJ = 1024
TM_MLP = 512
TQ_MLA = 2048
TK_MLA = 512
TQ_SB = 1024
TK_SB = 256
LOG2_E = math.log2(math.e)
FF_CHUNK = 1024


def _params(*sem):
    return pltpu.CompilerParams(dimension_semantics=sem, vmem_limit_bytes=VMEM_LIMIT)


def _resident(shape):
    nd = len(shape)
    return pl.BlockSpec(shape, lambda *_: (0,) * nd, pipeline_mode=pl.Buffered(1))


def _rms(xf, g):
    ms = jnp.mean(xf * xf, axis=-1, keepdims=True)
    return xf * lax.rsqrt(ms + EPS) * g


def _dot(a, b):
    return jnp.dot(a, b, preferred_element_type=F32)


def _dot_nt(a, b):
    return lax.dot_general(a, b, (((1,), (1,)), ((), ())), preferred_element_type=F32)


def _mla_proj_kernel(x_ref, posr_ref, g_ref, wd_ref, nq_ref, wuqt_ref, nkv_ref, wuk_ref, wuvt_ref,
                     invfc_ref, qt_ref, k_ref, vt_ref, *, scale):
    h = _rms(x_ref[0], g_ref[...]).astype(BF16)
    d = _dot(h, wd_ref[...])
    cq = _rms(d[:, :MLA_Q_RANK], nq_ref[...]).astype(BF16)
    ckv = _rms(d[:, MLA_Q_RANK:MLA_Q_RANK + MLA_KV_RANK], nkv_ref[...]).astype(BF16)
    kr = d[:, MLA_Q_RANK + MLA_KV_RANK:]

    half = MLA_ROPE // 2
    ang_t = invfc_ref[...] * posr_ref[0].astype(F32)
    cos_t = jnp.cos(ang_t)
    sin_t = jnp.sin(ang_t)

    kr_t = jnp.transpose(kr)
    k1, k2 = kr_t[MLA_NOPE:MLA_NOPE + half], kr_t[MLA_NOPE + half:MLA_NOPE + MLA_ROPE]
    tm = kr.shape[0]
    kr = jnp.transpose(jnp.concatenate(
        [jnp.zeros((MLA_NOPE, tm), F32), k1 * cos_t - k2 * sin_t, k2 * cos_t + k1 * sin_t,
         jnp.zeros((HEAD_SLAB - MLA_NOPE - MLA_ROPE, tm), F32)], axis=0))
    kf = _dot(ckv, wuk_ref[...])
    nope_lanes = lax.broadcasted_iota(jnp.int32, (1, LANES), 1) < MLA_NOPE
    for pair in range(MLA_HEADS // 2):
        both = kf[:, pair * LANES:(pair + 1) * LANES]
        for odd, lanes in enumerate((both, pltpu.roll(both, LANES - MLA_NOPE, 1))):
            hd = 2 * pair + odd
            k_ref[0, :, hd * HEAD_SLAB:(hd + 1) * HEAD_SLAB] = jnp.where(nope_lanes, lanes, kr).astype(BF16)
    vt_ref[0] = _dot_nt(wuvt_ref[...], ckv).astype(BF16)

    qft = _dot_nt(wuqt_ref[...], cq)
    cos_t = cos_t * scale
    sin_t = sin_t * scale
    hd_rows = MLA_NOPE + MLA_ROPE
    for hd in range(MLA_HEADS):
        s0 = hd * hd_rows
        s1, s2, s3 = s0 + MLA_NOPE, s0 + MLA_NOPE + half, s0 + hd_rows
        r0 = hd * HEAD_SLAB
        r1, r2, r3 = r0 + MLA_NOPE, r0 + MLA_NOPE + half, r0 + hd_rows
        x1, x2 = qft[s1:s2], qft[s2:s3]
        qt_ref[0, r0:r1] = (qft[s0:s1] * scale).astype(BF16)
        qt_ref[0, r1:r2] = (x1 * cos_t - x2 * sin_t).astype(BF16)
        qt_ref[0, r2:r3] = (x2 * cos_t + x1 * sin_t).astype(BF16)
        qt_ref[0, r3:r0 + HEAD_SLAB] = jnp.zeros((HEAD_SLAB - hd_rows, qft.shape[1]), BF16)


def _mla_proj(x, posr, g, wd, nq, wuqt, nkv, wuk, wuvt, invfc):
    b, s, dm = x.shape
    tm = TM_PROJ
    hw = MLA_HEADS * HEAD_SLAB
    vw = MLA_HEADS * MLA_V
    tok = lambda w: pl.BlockSpec((1, tm, w), lambda bi, si: (bi, si, 0))
    tok_t = lambda r: pl.BlockSpec((1, r, tm), lambda bi, si: (bi, 0, si))
    return pl.pallas_call(
        functools.partial(_mla_proj_kernel, scale=LOG2_E / math.sqrt(MLA_NOPE + MLA_ROPE)),
        grid=(b, s // tm),
        in_specs=[tok(dm), tok_t(1), _resident(g.shape), _resident(wd.shape), _resident(nq.shape),
                  _resident(wuqt.shape), _resident(nkv.shape), _resident(wuk.shape),
                  _resident(wuvt.shape), _resident(invfc.shape)],
        out_specs=[tok_t(hw), tok(hw), tok_t(vw)],
        out_shape=[jax.ShapeDtypeStruct((b, hw, s), BF16), jax.ShapeDtypeStruct((b, s, hw), BF16),
                   jax.ShapeDtypeStruct((b, vw, s), BF16)],
        compiler_params=_params("arbitrary", "arbitrary"),
        name="mla_proj",
    )(x, posr, g, wd, nq, wuqt, nkv, wuk, wuvt, invfc)


def _mla_attn_kernel(qt_ref, k_ref, vt_ref, o_ref, m_scr, l_scr, acc_scr, sa_scr, sb_scr, *,
                     tq, tk):
    qi = pl.program_id(2)
    m_scr[...] = jnp.full(m_scr.shape, -jnp.inf, F32)
    l_scr[...] = jnp.zeros(l_scr.shape, F32)
    acc_scr[...] = jnp.zeros(acc_scr.shape, F32)

    half = tk // 2

    def scores(j, s_scr, q0=0, diagonal=False):
        ks = pl.multiple_of(j * tk, tk)
        for hd in range(2):
            sl = slice(hd * HEAD_SLAB, (hd + 1) * HEAD_SLAB)
            if diagonal:
                s_scr[hd, :half, q0:q0 + half] = _dot(k_ref[0, pl.ds(ks, half), sl], qt_ref[0, sl, q0:q0 + half])
                s_scr[hd, :, q0 + half:] = _dot(k_ref[0, pl.ds(ks, tk), sl], qt_ref[0, sl, q0 + half:])
            else:
                s_scr[hd, :, q0:] = _dot(k_ref[0, pl.ds(ks, tk), sl], qt_ref[0, sl, q0:])

    def finish(j, s_scr, diag=None):
        ks = pl.multiple_of(j * tk, tk)
        vtb = vt_ref[0, :, pl.ds(ks, tk)]
        if diag is None:
            chunks = [(0, tq, tk, None)]
        else:
            d0 = diag * tk
            chunks = [(d0, half, half, 0), (d0 + half, half, tk, half)]
            chunks += [(c * tk, tk, tk, None) for c in range(diag + 1, tq // tk)]
        ones = jnp.ones((BF16_ROWS, tk), BF16)
        for hd in range(2):
            rows = slice(hd * MLA_V, (hd + 1) * MLA_V)
            vt_ext = jnp.concatenate([vtb[rows, :], ones], axis=0)
            for q0, nq, nk, offset in chunks:
                cols = slice(q0, q0 + nq)
                st = s_scr[hd, :nk, cols]
                if offset is not None:
                    key = lax.broadcasted_iota(jnp.int32, st.shape, 0)
                    qry = lax.broadcasted_iota(jnp.int32, st.shape, 1) + offset
                    st = jnp.where(key <= qry, st, -jnp.inf)
                m_old = m_scr[hd, :, cols]
                m_new = jnp.maximum(m_old, jnp.max(st, axis=0, keepdims=True))
                alpha = jnp.exp2(m_old - m_new)
                pv = _dot(vt_ext[:, :nk], jnp.exp2(st - m_new).astype(BF16))
                l_scr[hd, :, cols] = alpha * l_scr[hd, :, cols] + pv[MLA_V:MLA_V + 1]
                m_scr[hd, :, cols] = m_new
                acc_scr[rows, cols] = acc_scr[rows, cols] * alpha + pv[:MLA_V]

    span = tq // tk
    assert span % 2 == 0
    n_full = span * qi
    bufs = (sa_scr, sb_scr)

    def own_span(then_full_tiles):
        scores(n_full + span - 1, sa_scr, q0=(span - 1) * tk, diagonal=True)
        for i, d in enumerate(reversed(range(span))):
            if d > 0:
                scores(n_full + d - 1, bufs[(i + 1) % 2], q0=(d - 1) * tk, diagonal=True)
            elif then_full_tiles:
                scores(0, bufs[(i + 1) % 2])
            finish(n_full + d, bufs[i % 2], diag=d)

    @pl.when(qi == 0)
    def _():
        own_span(False)

    @pl.when(qi > 0)
    def _():
        own_span(True)

        def body(t, carry):
            scores(2 * t + 1, sb_scr)
            finish(2 * t, sa_scr)
            scores(2 * t + 2, sa_scr)
            finish(2 * t + 1, sb_scr)
            return carry

        lax.fori_loop(0, n_full // 2 - 1, body, 0)
        scores(n_full - 1, sb_scr)
        finish(n_full - 2, sa_scr)
        finish(n_full - 1, sb_scr)

    for hd in range(2):
        rows = slice(hd * MLA_V, (hd + 1) * MLA_V)
        acc_scr[rows, :] = acc_scr[rows, :] * (1.0 / l_scr[hd])
    o_ref[0] = jnp.transpose(acc_scr[...]).astype(BF16)


def _mla_attn(qt, k, vt):
    b, s, _ = k.shape
    tq, tk = TQ_MLA, TK_MLA
    assert s % tq == 0 and tq % tk == 0
    pairs = MLA_HEADS // 2
    return pl.pallas_call(
        functools.partial(_mla_attn_kernel, tq=tq, tk=tk),
        grid=(b, pairs, s // tq),
        in_specs=[pl.BlockSpec((1, 2 * HEAD_SLAB, tq), lambda bi, pi, qi: (bi, pi, qi)),
                  pl.BlockSpec((1, s, 2 * HEAD_SLAB), lambda bi, pi, qi: (bi, 0, pi)),
                  pl.BlockSpec((1, 2 * MLA_V, s), lambda bi, pi, qi: (bi, pi, 0))],
        out_specs=pl.BlockSpec((1, tq, 2 * MLA_V), lambda bi, pi, qi: (bi, qi, pi)),
        out_shape=jax.ShapeDtypeStruct((b, s, MLA_HEADS * MLA_V), BF16),
        scratch_shapes=[pltpu.VMEM((2, 1, tq), F32), pltpu.VMEM((2, 1, tq), F32),
                        pltpu.VMEM((2 * MLA_V, tq), F32),
                        pltpu.VMEM((2, tk, tq), F32), pltpu.VMEM((2, tk, tq), F32)],
        compiler_params=_params("arbitrary", "arbitrary", "arbitrary"),
        name="mla_attn",
    )(qt, k, vt)


def _conv_kernel(x_ref, g_ref, win_ref, cw_ref, cb_ref, o_ref, tail_scr, *, tm, dm):
    @pl.when(pl.program_id(1) == 0)
    def _():
        tail_scr[...] = jnp.zeros(tail_scr.shape, F32)

    h = _rms(x_ref[0], g_ref[...]).astype(BF16)
    bcu = _dot(h, win_ref[...])
    gb = bcu[:, :dm]
    u = bcu[:, dm:2 * dm] * bcu[:, 2 * dm:]
    tail = tail_scr[...]
    row = lax.broadcasted_iota(jnp.int32, (tm, 1), 0)
    prev1 = jnp.where(row == 0, tail[7:8], pltpu.roll(u, 1, 0))
    prev2 = jnp.where(row == 0, tail[6:7], jnp.where(row == 1, tail[7:8], pltpu.roll(u, 2, 0)))
    y = cw_ref[0:1] * prev2 + cw_ref[1:2] * prev1 + cw_ref[2:3] * u + cb_ref[...]
    o_ref[0] = (gb * y).astype(BF16)
    tail_scr[...] = u[tm - 8:, :]


def _conv_pre(x, g, win, cw, cb):
    b, s, dm = x.shape
    tm = TM_PROJ
    tok = pl.BlockSpec((1, tm, dm), lambda bi, si: (bi, si, 0))
    return pl.pallas_call(
        functools.partial(_conv_kernel, tm=tm, dm=dm),
        grid=(b, s // tm),
        in_specs=[tok, _resident(g.shape), _resident(win.shape), _resident(cw.shape),
                  _resident(cb.shape)],
        out_specs=tok,
        out_shape=jax.ShapeDtypeStruct((b, s, dm), BF16),
        scratch_shapes=[pltpu.VMEM((8, dm), F32)],
        compiler_params=_params("arbitrary", "arbitrary"),
        name="conv_pre",
    )(x, g, win, cw, cb)


def _sb_proj_kernel(x_ref, g_ref, wk_ref, wqvt_ref, k_ref, qvt_ref, *, dm, scale):
    h = _rms(x_ref[0], g_ref[...]).astype(BF16)
    k_ref[0] = _dot(h, wk_ref[...]).astype(BF16)
    t = _dot_nt(wqvt_ref[...], h)
    qvt_ref[0, :dm] = (t[:dm] * scale).astype(BF16)
    qvt_ref[0, dm:] = t[dm:].astype(BF16)


def _sb_proj(x, g, wk, wqvt, scale):
    b, s, dm = x.shape
    tm = TM_PROJ
    return pl.pallas_call(
        functools.partial(_sb_proj_kernel, dm=dm, scale=scale),
        grid=(b, s // tm),
        in_specs=[pl.BlockSpec((1, tm, dm), lambda bi, si: (bi, si, 0)), _resident(g.shape),
                  _resident(wk.shape), _resident(wqvt.shape)],
        out_specs=[pl.BlockSpec((1, tm, dm), lambda bi, si: (bi, si, 0)),
                   pl.BlockSpec((1, 2 * dm, tm), lambda bi, si: (bi, 0, si))],
        out_shape=[jax.ShapeDtypeStruct((b, s, dm), BF16), jax.ShapeDtypeStruct((b, 2 * dm, s), BF16)],
        compiler_params=_params("arbitrary", "arbitrary"),
        name="sb_proj",
    )(x, g, wk, wqvt)


def _sb_attn_kernel(qt_ref, k_ref, vt_ref, ntri_ref, o_ref, run_scr, acc_scr, za_scr, zb_scr, qh_scr, lb_scr,
                    sm_scr, *,
                    tq, tk):
    span = tq // tk
    assert span % 2 == 0
    qi = pl.program_id(2)
    last = span * (qi + 1) - 1
    hd_w = LANES // 2
    low = lax.broadcasted_iota(jnp.int32, (LANES, 1), 0) < hd_w
    qt = qt_ref[0]
    zero_q = jnp.zeros_like(qt)
    qh_scr[0] = jnp.where(low, qt, zero_q)
    qh_scr[1] = jnp.where(low, zero_q, qt)
    run_scr[...] = jnp.zeros(run_scr.shape, F32)
    acc_scr[...] = jnp.zeros(acc_scr.shape, F32)

    def scores(u, z_scr, q0=0):
        ks = pl.multiple_of((last - u) * tk, tk)
        kb = k_ref[0, pl.ds(ks, tk), :]
        for hd in range(2):
            z_scr[hd, :, q0:] = _dot(kb, qh_scr[hd, :, q0:])

    def finish(u, z_scr, diag=None):
        ks = pl.multiple_of((last - u) * tk, tk)
        vtb = vt_ref[0, :, pl.ds(ks, tk)]
        kb = k_ref[0, pl.ds(ks, tk), :]
        q0 = 0 if diag is None else diag * tk
        cols = slice(q0, tq)

        def masses(hd, triangular):
            z = z_scr[hd, :, cols]
            neg_abs = lax.bitcast_convert_type(
                lax.bitcast_convert_type(z, jnp.uint32) | jnp.uint32(0x80000000), F32)
            softplus = jnp.maximum(z, 0.0) + jnp.log2(1.0 + jnp.exp2(neg_abs))
            keep = None
            mass = softplus
            if triangular:
                keep = (lax.broadcasted_iota(jnp.int32, z.shape, 0)
                        < lax.broadcasted_iota(jnp.int32, z.shape, 1))
                mass = jnp.where(keep, softplus, 0.0)
            mass16 = mass.astype(BF16)
            sm_scr[hd, :tk, cols] = _dot(ntri_ref[:tk], mass16) + _dot(kb, qh_scr[hd, :, cols])
            sm_scr[hd, tk:, cols] = _dot(ntri_ref[tk:], mass16)
            lb_scr[hd, :, cols] = softplus
            return (keep,)

        def weights(hd, keep):
            rows = slice(hd * hd_w, (hd + 1) * hd_w)
            sums = sm_scr[hd, :, cols]
            a = jnp.exp2(sums[:tk] - lb_scr[hd, :, cols])
            if keep is not None:
                a = jnp.where(keep, a, 0.0)
            run = run_scr[hd, :, cols]
            acc_scr[rows, cols] += _dot(vtb[rows, :], a.astype(BF16)) * jnp.exp2(run)
            run_scr[hd, :, cols] = run + sums[tk:tk + 1]

        staged = [masses(hd, diag is not None) for hd in range(2)]
        for hd in range(2):
            weights(hd, *staged[hd])

    bufs = (za_scr, zb_scr)
    n_full = span * qi

    def own_span(then_full_tiles):
        scores(0, za_scr, q0=(span - 1) * tk)
        for u in range(span):
            if u + 1 < span:
                scores(u + 1, bufs[(u + 1) % 2], q0=(span - 2 - u) * tk)
            elif then_full_tiles:
                scores(u + 1, bufs[(u + 1) % 2])
            finish(u, bufs[u % 2], diag=span - 1 - u)

    @pl.when(qi == 0)
    def _():
        own_span(False)

    @pl.when(qi > 0)
    def _():
        own_span(True)

        def body(t, carry):
            u = span + 2 * t
            scores(u + 1, zb_scr)
            finish(u, za_scr)
            scores(u + 2, za_scr)
            finish(u + 1, zb_scr)
            return carry

        lax.fori_loop(0, n_full // 2 - 1, body, 0)
        u = span + n_full - 2
        scores(u + 1, zb_scr)
        finish(u, za_scr)
        finish(u + 1, zb_scr)

    o_ref[0] = jnp.transpose(acc_scr[...]).astype(BF16)


def _sb_attn(k, qvt, ntri):
    b, s, dm = k.shape
    tq, tk = TQ_SB, TK_SB
    assert s % tq == 0 and tq % tk == 0
    pairs = SB_HEADS // 2
    return pl.pallas_call(
        functools.partial(_sb_attn_kernel, tq=tq, tk=tk),
        grid=(b, pairs, s // tq),
        in_specs=[pl.BlockSpec((1, LANES, tq), lambda bi, pi, qi: (bi, pi, qi)),
                  pl.BlockSpec((1, s, LANES), lambda bi, pi, qi: (bi, 0, pi)),
                  pl.BlockSpec((1, LANES, s), lambda bi, pi, qi: (bi, pairs + pi, 0)),
                  _resident(ntri.shape)],
        out_specs=pl.BlockSpec((1, tq, LANES), lambda bi, pi, qi: (bi, qi, pi)),
        out_shape=jax.ShapeDtypeStruct((b, s, dm), BF16),
        scratch_shapes=[pltpu.VMEM((2, 1, tq), F32), pltpu.VMEM((LANES, tq), F32),
                        pltpu.VMEM((2, tk, tq), F32), pltpu.VMEM((2, tk, tq), F32),
                        pltpu.VMEM((2, LANES, tq), BF16), pltpu.VMEM((2, tk, tq), F32),
                        pltpu.VMEM((2, tk + BF16_ROWS, tq), F32)],
        compiler_params=_params("arbitrary", "arbitrary", "arbitrary"),
        name="sb_attn",
    )(qvt, k, qvt, ntri)


def _mlp_kernel(x_ref, a_ref, wa_ref, g_ref, wup_ref, wdn_ref, gf_ref, o_ref, *, d_ff, final):
    x1 = x_ref[...] + _dot(a_ref[...], wa_ref[...])
    h = _rms(x1, g_ref[...]).astype(BF16)
    acc = x1
    for c in range(d_ff // FF_CHUNK):
        sl = slice(c * FF_CHUNK, (c + 1) * FF_CHUNK)
        r = jnp.maximum(_dot(h, wup_ref[:, sl]), 0.0)
        acc = acc + _dot((r * r).astype(BF16), wdn_ref[sl, :])
    if final:
        acc = _rms(acc, gf_ref[...])
    o_ref[...] = acc


def _mlp(x2, a2, wa, g, wup, wdn, gf, final):
    t, dm = x2.shape
    tm = TM_MLP
    tok = pl.BlockSpec((tm, dm), lambda i: (i, 0))
    return pl.pallas_call(
        functools.partial(_mlp_kernel, d_ff=wup.shape[1], final=final),
        grid=(t // tm,),
        in_specs=[tok, tok, _resident(wa.shape), _resident(g.shape), _resident(wup.shape),
                  _resident(wdn.shape), _resident(gf.shape)],
        out_specs=tok,
        out_shape=jax.ShapeDtypeStruct((t, dm), F32),
        compiler_params=_params("arbitrary"),
        name="out_proj_mlp",
    )(x2, a2, wa, g, wup, wdn, gf)


def _row(v):
    return v.reshape(1, -1).astype(F32)


def _mla_layer(x, posr, invfc, norm_mix, w_dq, norm_q, w_uq, w_dkv, norm_kv, w_uk, w_uv):
    w_kr = jnp.pad(w_dkv[:, MLA_KV_RANK:], ((0, 0), (MLA_NOPE, HEAD_SLAB - MLA_NOPE - MLA_ROPE)))
    wd = jnp.concatenate([w_dq, w_dkv[:, :MLA_KV_RANK], w_kr], axis=1).astype(BF16)
    wuqt = w_uq.T.astype(BF16)
    wuk = w_uk.astype(BF16)
    qt, k, vt = _mla_proj(x, posr, _row(norm_mix), wd, _row(norm_q), wuqt, _row(norm_kv), wuk,
                          w_uv.T.astype(BF16), invfc)
    return _mla_attn(qt, k, vt)


def kernel(x, positions, l0_norm_mix, l0_w_dq, l0_norm_q, l0_w_uq, l0_w_dkv, l0_norm_kv, l0_w_uk, l0_w_uv, l0_w_o, l0_norm_mlp, l0_w_up, l0_w_down, l1_norm_mix, l1_w_in, l1_conv_w, l1_conv_b, l1_w_out, l1_norm_mlp, l1_w_up, l1_w_down, l2_norm_mix, l2_w_qkv, l2_w_o, l2_norm_mlp, l2_w_up, l2_w_down, l3_norm_mix, l3_w_dq, l3_norm_q, l3_w_uq, l3_w_dkv, l3_norm_kv, l3_w_uk, l3_w_uv, l3_w_o, l3_norm_mlp, l3_w_up, l3_w_down, final_norm):
    b, s, dm = x.shape
    t = b * s
    assert s % TM_PROJ == 0
    posr = positions.reshape(b, 1, s)
    inv_freq = ROPE_THETA ** (-jnp.arange(0, MLA_ROPE, 2, dtype=F32) / MLA_ROPE)
    invfc = inv_freq.reshape(-1, 1)
    tri_row = jnp.arange(TK_SB + BF16_ROWS)[:, None]
    ntri = -((jnp.arange(TK_SB)[None, :] > tri_row) | (tri_row >= TK_SB)).astype(BF16)
    gf = _row(final_norm)

    def mlp(xc, a, wa, g, wup, wdn, final=False):
        out = _mlp(xc.reshape(t, dm), a.reshape(t, dm), wa.astype(BF16), _row(g), wup.astype(BF16),
                   wdn.astype(BF16), gf, final)
        return out.reshape(b, s, dm)

    o = _mla_layer(x, posr, invfc, l0_norm_mix, l0_w_dq, l0_norm_q, l0_w_uq, l0_w_dkv, l0_norm_kv,
                   l0_w_uk, l0_w_uv)
    x = mlp(x, o, l0_w_o, l0_norm_mlp, l0_w_up, l0_w_down)

    gy = _conv_pre(x, _row(l1_norm_mix), l1_w_in.astype(BF16), l1_conv_w.astype(F32), _row(l1_conv_b))
    x = mlp(x, gy, l1_w_out, l1_norm_mlp, l1_w_up, l1_w_down)

    w_qv = jnp.concatenate([l2_w_qkv[:, :dm], l2_w_qkv[:, 2 * dm:]], axis=1)
    k, qvt = _sb_proj(x, _row(l2_norm_mix), l2_w_qkv[:, dm:2 * dm].astype(BF16), w_qv.T.astype(BF16),
                      LOG2_E / math.sqrt(dm // SB_HEADS))
    o = _sb_attn(k, qvt, ntri)
    x = mlp(x, o, l2_w_o, l2_norm_mlp, l2_w_up, l2_w_down)

    o = _mla_layer(x, posr, invfc, l3_norm_mix, l3_w_dq, l3_norm_q, l3_w_uq, l3_w_dkv, l3_norm_kv,
                   l3_w_uk, l3_w_uv)
    return mlp(x, o, l3_w_o, l3_norm_mlp, l3_w_up, l3_w_down, final=True)
```

```python
import functools
import math

import jax
import jax.numpy as jnp
from jax import lax
from jax.experimental import pallas as pl
from jax.experimental.pallas import tpu as pltpu

F32 = jnp.float32
BF16 = jnp.bfloat16

EPS = 1e-6
ROPE_THETA = 10000.0
MLA_HEADS = 16
MLA_NOPE = 64
MLA_ROPE = 32
MLA_V = 64
MLA_Q_RANK = 384
MLA_KV_RANK = 256
SB_HEADS = 16

LANES = 128
BF16_ROWS = 16
HEAD_SLAB = LANES
VMEM_LIMIT = 56 * 1024 * 1024

TM_PROJ = 1024
TM_MLP = 512
TQ_MLA = 2048
TK_MLA = 512
TQ_SB = 1024
TK_SB = 256
LOG2_E = math.log2(math.e)
FF_CHUNK = 1024


def _params(*sem):
    return pltpu.CompilerParams(dimension_semantics=sem, vmem_limit_bytes=VMEM_LIMIT)


def _resident(shape):
    nd = len(shape)
    return pl.BlockSpec(shape, lambda *_: (0,) * nd, pipeline_mode=pl.Buffered(1))


def _rms(xf, g):
    ms = jnp.mean(xf * xf, axis=-1, keepdims=True)
    return xf * lax.rsqrt(ms + EPS) * g


def _dot(a, b):
    return jnp.dot(a, b, preferred_element_type=F32)


def _dot_nt(a, b):
    return lax.dot_general(a, b, (((1,), (1,)), ((), ())), preferred_element_type=F32)


def _mla_proj_kernel(x_ref, posr_ref, g_ref, wd_ref, nq_ref, wuqt_ref, nkv_ref, wuk_ref, wuvt_ref,
                     invfc_ref, qt_ref, k_ref, vt_ref, *, scale):
    h = _rms(x_ref[0], g_ref[...]).astype(BF16)
    d = _dot(h, wd_ref[...])
    cq = _rms(d[:, :MLA_Q_RANK], nq_ref[...]).astype(BF16)
    ckv = _rms(d[:, MLA_Q_RANK:MLA_Q_RANK + MLA_KV_RANK], nkv_ref[...]).astype(BF16)
    kr = d[:, MLA_Q_RANK + MLA_KV_RANK:]

    half = MLA_ROPE // 2
    ang_t = invfc_ref[...] * posr_ref[0].astype(F32)
    cos_t = jnp.cos(ang_t)
    sin_t = jnp.sin(ang_t)

    kr_t = jnp.transpose(kr)
    k1, k2 = kr_t[MLA_NOPE:MLA_NOPE + half], kr_t[MLA_NOPE + half:MLA_NOPE + MLA_ROPE]
    tm = kr.shape[0]
    kr = jnp.transpose(jnp.concatenate(
        [jnp.zeros((MLA_NOPE, tm), F32), k1 * cos_t - k2 * sin_t, k2 * cos_t + k1 * sin_t,
         jnp.zeros((HEAD_SLAB - MLA_NOPE - MLA_ROPE, tm), F32)], axis=0))
    kf = _dot(ckv, wuk_ref[...])
    nope_lanes = lax.broadcasted_iota(jnp.int32, (1, LANES), 1) < MLA_NOPE
    for pair in range(MLA_HEADS // 2):
        both = kf[:, pair * LANES:(pair + 1) * LANES]
        for odd, lanes in enumerate((both, pltpu.roll(both, LANES - MLA_NOPE, 1))):
            hd = 2 * pair + odd
            k_ref[0, :, hd * HEAD_SLAB:(hd + 1) * HEAD_SLAB] = jnp.where(nope_lanes, lanes, kr).astype(BF16)
    vt_ref[0] = _dot_nt(wuvt_ref[...], ckv).astype(BF16)

    qft = _dot_nt(wuqt_ref[...], cq)
    cos_t = cos_t * scale
    sin_t = sin_t * scale
    hd_rows = MLA_NOPE + MLA_ROPE
    for hd in range(MLA_HEADS):
        s0 = hd * hd_rows
        s1, s2, s3 = s0 + MLA_NOPE, s0 + MLA_NOPE + half, s0 + hd_rows
        r0 = hd * HEAD_SLAB
        r1, r2, r3 = r0 + MLA_NOPE, r0 + MLA_NOPE + half, r0 + hd_rows
        x1, x2 = qft[s1:s2], qft[s2:s3]
        qt_ref[0, r0:r1] = (qft[s0:s1] * scale).astype(BF16)
        qt_ref[0, r1:r2] = (x1 * cos_t - x2 * sin_t).astype(BF16)
        qt_ref[0, r2:r3] = (x2 * cos_t + x1 * sin_t).astype(BF16)
        qt_ref[0, r3:r0 + HEAD_SLAB] = jnp.zeros((HEAD_SLAB - hd_rows, qft.shape[1]), BF16)


def _mla_proj(x, posr, g, wd, nq, wuqt, nkv, wuk, wuvt, invfc):
    b, s, dm = x.shape
    tm = TM_PROJ
    hw = MLA_HEADS * HEAD_SLAB
    vw = MLA_HEADS * MLA_V
    tok = lambda w: pl.BlockSpec((1, tm, w), lambda bi, si: (bi, si, 0))
    tok_t = lambda r: pl.BlockSpec((1, r, tm), lambda bi, si: (bi, 0, si))
    return pl.pallas_call(
        functools.partial(_mla_proj_kernel, scale=LOG2_E / math.sqrt(MLA_NOPE + MLA_ROPE)),
        grid=(b, s // tm),
        in_specs=[tok(dm), tok_t(1), _resident(g.shape), _resident(wd.shape), _resident(nq.shape),
                  _resident(wuqt.shape), _resident(nkv.shape), _resident(wuk.shape),
                  _resident(wuvt.shape), _resident(invfc.shape)],
        out_specs=[tok_t(hw), tok(hw), tok_t(vw)],
        out_shape=[jax.ShapeDtypeStruct((b, hw, s), BF16), jax.ShapeDtypeStruct((b, s, hw), BF16),
                   jax.ShapeDtypeStruct((b, vw, s), BF16)],
        compiler_params=_params("arbitrary", "arbitrary"),
        name="mla_proj",
    )(x, posr, g, wd, nq, wuqt, nkv, wuk, wuvt, invfc)


def _mla_attn_kernel(qt_ref, k_ref, vt_ref, o_ref, m_scr, l_scr, acc_scr, sa_scr, sb_scr, *,
                     tq, tk):
    qi = pl.program_id(2)
    m_scr[...] = jnp.full(m_scr.shape, -jnp.inf, F32)
    l_scr[...] = jnp.zeros(l_scr.shape, F32)
    acc_scr[...] = jnp.zeros(acc_scr.shape, F32)

    half = tk // 2

    def scores(j, s_scr, q0=0, diagonal=False):
        ks = pl.multiple_of(j * tk, tk)
        for hd in range(2):
            sl = slice(hd * HEAD_SLAB, (hd + 1) * HEAD_SLAB)
            if diagonal:
                s_scr[hd, :half, q0:q0 + half] = _dot(k_ref[0, pl.ds(ks, half), sl], qt_ref[0, sl, q0:q0 + half])
                s_scr[hd, :, q0 + half:] = _dot(k_ref[0, pl.ds(ks, tk), sl], qt_ref[0, sl, q0 + half:])
            else:
                s_scr[hd, :, q0:] = _dot(k_ref[0, pl.ds(ks, tk), sl], qt_ref[0, sl, q0:])

    def finish(j, s_scr, diag=None):
        ks = pl.multiple_of(j * tk, tk)
        vtb = vt_ref[0, :, pl.ds(ks, tk)]
        if diag is None:
            chunks = [(0, tq, tk, None)]
        else:
            d0 = diag * tk
            chunks = [(d0, half, half, 0), (d0 + half, half, tk, half)]
            chunks += [(c * tk, tk, tk, None) for c in range(diag + 1, tq // tk)]
        ones = jnp.ones((BF16_ROWS, tk), BF16)
        for hd in range(2):
            rows = slice(hd * MLA_V, (hd + 1) * MLA_V)
            vt_ext = jnp.concatenate([vtb[rows, :], ones], axis=0)
            for q0, nq, nk, offset in chunks:
                cols = slice(q0, q0 + nq)
                st = s_scr[hd, :nk, cols]
                if offset is not None:
                    key = lax.broadcasted_iota(jnp.int32, st.shape, 0)
                    qry = lax.broadcasted_iota(jnp.int32, st.shape, 1) + offset
                    st = jnp.where(key <= qry, st, -jnp.inf)
                m_old = m_scr[hd, :, cols]
                m_new = jnp.maximum(m_old, jnp.max(st, axis=0, keepdims=True))
                alpha = jnp.exp2(m_old - m_new)
                pv = _dot(vt_ext[:, :nk], jnp.exp2(st - m_new).astype(BF16))
                l_scr[hd, :, cols] = alpha * l_scr[hd, :, cols] + pv[MLA_V:MLA_V + 1]
                m_scr[hd, :, cols] = m_new
                acc_scr[rows, cols] = acc_scr[rows, cols] * alpha + pv[:MLA_V]

    span = tq // tk
    assert span % 2 == 0
    n_full = span * qi
    bufs = (sa_scr, sb_scr)

    def own_span(then_full_tiles):
        scores(n_full + span - 1, sa_scr, q0=(span - 1) * tk, diagonal=True)
        for i, d in enumerate(reversed(range(span))):
            if d > 0:
                scores(n_full + d - 1, bufs[(i + 1) % 2], q0=(d - 1) * tk, diagonal=True)
            elif then_full_tiles:
                scores(0, bufs[(i + 1) % 2])
            finish(n_full + d, bufs[i % 2], diag=d)

    @pl.when(qi == 0)
    def _():
        own_span(False)

    @pl.when(qi > 0)
    def _():
        own_span(True)

        def body(t, carry):
            scores(2 * t + 1, sb_scr)
            finish(2 * t, sa_scr)
            scores(2 * t + 2, sa_scr)
            finish(2 * t + 1, sb_scr)
            return carry

        lax.fori_loop(0, n_full // 2 - 1, body, 0)
        scores(n_full - 1, sb_scr)
        finish(n_full - 2, sa_scr)
        finish(n_full - 1, sb_scr)

    for hd in range(2):
        rows = slice(hd * MLA_V, (hd + 1) * MLA_V)
        acc_scr[rows, :] = acc_scr[rows, :] * (1.0 / l_scr[hd])
    o_ref[0] = jnp.transpose(acc_scr[...]).astype(BF16)


def _mla_attn(qt, k, vt):
    b, s, _ = k.shape
    tq, tk = TQ_MLA, TK_MLA
    assert s % tq == 0 and tq % tk == 0
    pairs = MLA_HEADS // 2
    return pl.pallas_call(
        functools.partial(_mla_attn_kernel, tq=tq, tk=tk),
        grid=(b, pairs, s // tq),
        in_specs=[pl.BlockSpec((1, 2 * HEAD_SLAB, tq), lambda bi, pi, qi: (bi, pi, qi)),
                  pl.BlockSpec((1, s, 2 * HEAD_SLAB), lambda bi, pi, qi: (bi, 0, pi)),
                  pl.BlockSpec((1, 2 * MLA_V, s), lambda bi, pi, qi: (bi, pi, 0))],
        out_specs=pl.BlockSpec((1, tq, 2 * MLA_V), lambda bi, pi, qi: (bi, qi, pi)),
        out_shape=jax.ShapeDtypeStruct((b, s, MLA_HEADS * MLA_V), BF16),
        scratch_shapes=[pltpu.VMEM((2, 1, tq), F32), pltpu.VMEM((2, 1, tq), F32),
                        pltpu.VMEM((2 * MLA_V, tq), F32),
                        pltpu.VMEM((2, tk, tq), F32), pltpu.VMEM((2, tk, tq), F32)],
        compiler_params=_params("arbitrary", "arbitrary", "arbitrary"),
        name="mla_attn",
    )(qt, k, vt)


def _conv_kernel(x_ref, g_ref, win_ref, cw_ref, cb_ref, o_ref, tail_scr, *, tm, dm):
    @pl.when(pl.program_id(1) == 0)
    def _():
        tail_scr[...] = jnp.zeros(tail_scr.shape, F32)

    h = _rms(x_ref[0], g_ref[...]).astype(BF16)
    bcu = _dot(h, win_ref[...])
    gb = bcu[:, :dm]
    u = bcu[:, dm:2 * dm] * bcu[:, 2 * dm:]
    tail = tail_scr[...]
    row = lax.broadcasted_iota(jnp.int32, (tm, 1), 0)
    prev1 = jnp.where(row == 0, tail[7:8], pltpu.roll(u, 1, 0))
    prev2 = jnp.where(row == 0, tail[6:7], jnp.where(row == 1, tail[7:8], pltpu.roll(u, 2, 0)))
    y = cw_ref[0:1] * prev2 + cw_ref[1:2] * prev1 + cw_ref[2:3] * u + cb_ref[...]
    o_ref[0] = (gb * y).astype(BF16)
    tail_scr[...] = u[tm - 8:, :]


def _conv_pre(x, g, win, cw, cb):
    b, s, dm = x.shape
    tm = TM_PROJ
    tok = pl.BlockSpec((1, tm, dm), lambda bi, si: (bi, si, 0))
    return pl.pallas_call(
        functools.partial(_conv_kernel, tm=tm, dm=dm),
        grid=(b, s // tm),
        in_specs=[tok, _resident(g.shape), _resident(win.shape), _resident(cw.shape),
                  _resident(cb.shape)],
        out_specs=tok,
        out_shape=jax.ShapeDtypeStruct((b, s, dm), BF16),
        scratch_shapes=[pltpu.VMEM((8, dm), F32)],
        compiler_params=_params("arbitrary", "arbitrary"),
        name="conv_pre",
    )(x, g, win, cw, cb)


def _sb_proj_kernel(x_ref, g_ref, wk_ref, wqvt_ref, k_ref, qvt_ref, *, dm, scale):
    h = _rms(x_ref[0], g_ref[...]).astype(BF16)
    k_ref[0] = _dot(h, wk_ref[...]).astype(BF16)
    t = _dot_nt(wqvt_ref[...], h)
    qvt_ref[0, :dm] = (t[:dm] * scale).astype(BF16)
    qvt_ref[0, dm:] = t[dm:].astype(BF16)


def _sb_proj(x, g, wk, wqvt, scale):
    b, s, dm = x.shape
    tm = TM_PROJ
    return pl.pallas_call(
        functools.partial(_sb_proj_kernel, dm=dm, scale=scale),
        grid=(b, s // tm),
        in_specs=[pl.BlockSpec((1, tm, dm), lambda bi, si: (bi, si, 0)), _resident(g.shape),
                  _resident(wk.shape), _resident(wqvt.shape)],
        out_specs=[pl.BlockSpec((1, tm, dm), lambda bi, si: (bi, si, 0)),
                   pl.BlockSpec((1, 2 * dm, tm), lambda bi, si: (bi, 0, si))],
        out_shape=[jax.ShapeDtypeStruct((b, s, dm), BF16), jax.ShapeDtypeStruct((b, 2 * dm, s), BF16)],
        compiler_params=_params("arbitrary", "arbitrary"),
        name="sb_proj",
    )(x, g, wk, wqvt)


def _sb_attn_kernel(qt_ref, k_ref, vt_ref, ntri_ref, o_ref, run_scr, acc_scr, za_scr, zb_scr, qh_scr, lb_scr, *,
                    tq, tk):
    span = tq // tk
    assert span % 2 == 0
    qi = pl.program_id(2)
    last = span * (qi + 1) - 1
    hd_w = LANES // 2
    low = lax.broadcasted_iota(jnp.int32, (LANES, 1), 0) < hd_w
    qt = qt_ref[0]
    zero_q = jnp.zeros_like(qt)
    qh_scr[0] = jnp.where(low, qt, zero_q)
    qh_scr[1] = jnp.where(low, zero_q, qt)
    run_scr[...] = jnp.zeros(run_scr.shape, F32)
    acc_scr[...] = jnp.zeros(acc_scr.shape, F32)

    def scores(u, z_scr, q0=0):
        ks = pl.multiple_of((last - u) * tk, tk)
        kb = k_ref[0, pl.ds(ks, tk), :]
        for hd in range(2):
            z_scr[hd, :, q0:] = _dot(kb, qh_scr[hd, :, q0:])

    def finish(u, z_scr, diag=None):
        ks = pl.multiple_of((last - u) * tk, tk)
        vtb = vt_ref[0, :, pl.ds(ks, tk)]
        q0 = 0 if diag is None else diag * tk
        cols = slice(q0, tq)

        def masses(hd, triangular):
            z = z_scr[hd, :, cols]
            neg_abs = lax.bitcast_convert_type(
                lax.bitcast_convert_type(z, jnp.uint32) | jnp.uint32(0x80000000), F32)
            softplus = jnp.maximum(z, 0.0) + jnp.log2(1.0 + jnp.exp2(neg_abs))
            keep = None
            mass = softplus
            if triangular:
                keep = (lax.broadcasted_iota(jnp.int32, z.shape, 0)
                        < lax.broadcasted_iota(jnp.int32, z.shape, 1))
                mass = jnp.where(keep, softplus, 0.0)
            ntri = ntri_ref[...]
            sums = _dot(ntri, mass.astype(BF16))
            lb_scr[hd, :, cols] = z - softplus
            return keep, sums

        def weights(hd, keep, sums):
            rows = slice(hd * hd_w, (hd + 1) * hd_w)
            a = jnp.exp2(lb_scr[hd, :, cols] + sums[:tk])
            if keep is not None:
                a = jnp.where(keep, a, 0.0)
            run = run_scr[hd, :, cols]
            acc_scr[rows, cols] += _dot(vtb[rows, :], a.astype(BF16)) * jnp.exp2(run)
            run_scr[hd, :, cols] = run + sums[tk:tk + 1]

        staged = [masses(hd, diag is not None) for hd in range(2)]
        for hd in range(2):
            weights(hd, *staged[hd])

    bufs = (za_scr, zb_scr)
    n_full = span * qi

    def own_span(then_full_tiles):
        scores(0, za_scr, q0=(span - 1) * tk)
        for u in range(span):
            if u + 1 < span:
                scores(u + 1, bufs[(u + 1) % 2], q0=(span - 2 - u) * tk)
            elif then_full_tiles:
                scores(u + 1, bufs[(u + 1) % 2])
            finish(u, bufs[u % 2], diag=span - 1 - u)

    @pl.when(qi == 0)
    def _():
        own_span(False)

    @pl.when(qi > 0)
    def _():
        own_span(True)

        def body(t, carry):
            u = span + 2 * t
            scores(u + 1, zb_scr)
            finish(u, za_scr)
            scores(u + 2, za_scr)
            finish(u + 1, zb_scr)
            return carry

        lax.fori_loop(0, n_full // 2 - 1, body, 0)
        u = span + n_full - 2
        scores(u + 1, zb_scr)
        finish(u, za_scr)
        finish(u + 1, zb_scr)

    o_ref[0] = jnp.transpose(acc_scr[...]).astype(BF16)


def _sb_attn(k, qvt, ntri):
    b, s, dm = k.shape
    tq, tk = TQ_SB, TK_SB
    assert s % tq == 0 and tq % tk == 0
    pairs = SB_HEADS // 2
    return pl.pallas_call(
        functools.partial(_sb_attn_kernel, tq=tq, tk=tk),
        grid=(b, pairs, s // tq),
        in_specs=[pl.BlockSpec((1, LANES, tq), lambda bi, pi, qi: (bi, pi, qi)),
                  pl.BlockSpec((1, s, LANES), lambda bi, pi, qi: (bi, 0, pi)),
                  pl.BlockSpec((1, LANES, s), lambda bi, pi, qi: (bi, pairs + pi, 0)),
                  _resident(ntri.shape)],
        out_specs=pl.BlockSpec((1, tq, LANES), lambda bi, pi, qi: (bi, qi, pi)),
        out_shape=jax.ShapeDtypeStruct((b, s, dm), BF16),
        scratch_shapes=[pltpu.VMEM((2, 1, tq), F32), pltpu.VMEM((LANES, tq), F32),
                        pltpu.VMEM((2, tk, tq), F32), pltpu.VMEM((2, tk, tq), F32),
                        pltpu.VMEM((2, LANES, tq), BF16), pltpu.VMEM((2, tk, tq), F32)],
        compiler_params=_params("arbitrary", "arbitrary", "arbitrary"),
        name="sb_attn",
    )(qvt, k, qvt, ntri)


def _mlp_kernel(x_ref, a_ref, wa_ref, g_ref, wup_ref, wdn_ref, gf_ref, o_ref, *, d_ff, final):
    x1 = x_ref[...] + _dot(a_ref[...], wa_ref[...])
    h = _rms(x1, g_ref[...]).astype(BF16)
    acc = x1
    for c in range(d_ff // FF_CHUNK):
        sl = slice(c * FF_CHUNK, (c + 1) * FF_CHUNK)
        r = jnp.maximum(_dot(h, wup_ref[:, sl]), 0.0)
        acc = acc + _dot((r * r).astype(BF16), wdn_ref[sl, :])
    if final:
        acc = _rms(acc, gf_ref[...])
    o_ref[...] = acc


def _mlp(x2, a2, wa, g, wup, wdn, gf, final):
    t, dm = x2.shape
    tm = TM_MLP
    tok = pl.BlockSpec((tm, dm), lambda i: (i, 0))
    return pl.pallas_call(
        functools.partial(_mlp_kernel, d_ff=wup.shape[1], final=final),
        grid=(t // tm,),
        in_specs=[tok, tok, _resident(wa.shape), _resident(g.shape), _resident(wup.shape),
                  _resident(wdn.shape), _resident(gf.shape)],
        out_specs=tok,
        out_shape=jax.ShapeDtypeStruct((t, dm), F32),
        compiler_params=_params("arbitrary"),
        name="out_proj_mlp",
    )(x2, a2, wa, g, wup, wdn, gf)


def _row(v):
    return v.reshape(1, -1).astype(F32)


def _mla_layer(x, posr, invfc, norm_mix, w_dq, norm_q, w_uq, w_dkv, norm_kv, w_uk, w_uv):
    w_kr = jnp.pad(w_dkv[:, MLA_KV_RANK:], ((0, 0), (MLA_NOPE, HEAD_SLAB - MLA_NOPE - MLA_ROPE)))
    wd = jnp.concatenate([w_dq, w_dkv[:, :MLA_KV_RANK], w_kr], axis=1).astype(BF16)
    wuqt = w_uq.T.astype(BF16)
    wuk = w_uk.astype(BF16)
    qt, k, vt = _mla_proj(x, posr, _row(norm_mix), wd, _row(norm_q), wuqt, _row(norm_kv), wuk,
                          w_uv.T.astype(BF16), invfc)
    return _mla_attn(qt, k, vt)


def kernel(x, positions, l0_norm_mix, l0_w_dq, l0_norm_q, l0_w_uq, l0_w_dkv, l0_norm_kv, l0_w_uk, l0_w_uv, l0_w_o, l0_norm_mlp, l0_w_up, l0_w_down, l1_norm_mix, l1_w_in, l1_conv_w, l1_conv_b, l1_w_out, l1_norm_mlp, l1_w_up, l1_w_down, l2_norm_mix, l2_w_qkv, l2_w_o, l2_norm_mlp, l2_w_up, l2_w_down, l3_norm_mix, l3_w_dq, l3_norm_q, l3_w_uq, l3_w_dkv, l3_norm_kv, l3_w_uk, l3_w_uv, l3_w_o, l3_norm_mlp, l3_w_up, l3_w_down, final_norm):
    b, s, dm = x.shape
    t = b * s
    assert s % TM_PROJ == 0
    posr = positions.reshape(b, 1, s)
    inv_freq = ROPE_THETA ** (-jnp.arange(0, MLA_ROPE, 2, dtype=F32) / MLA_ROPE)
    invfc = inv_freq.reshape(-1, 1)
    tri_row = jnp.arange(TK_SB + BF16_ROWS)[:, None]
    ntri = -((jnp.arange(TK_SB)[None, :] > tri_row) | (tri_row >= TK_SB)).astype(BF16)
    gf = _row(final_norm)

    def mlp(xc, a, wa, g, wup, wdn, final=False):
        out = _mlp(xc.reshape(t, dm), a.reshape(t, dm), wa.astype(BF16), _row(g), wup.astype(BF16),
                   wdn.astype(BF16), gf, final)
        return out.reshape(b, s, dm)

    o = _mla_layer(x, posr, invfc, l0_norm_mix, l0_w_dq, l0_norm_q, l0_w_uq, l0_w_dkv, l0_norm_kv,
                   l0_w_uk, l0_w_uv)
    x = mlp(x, o, l0_w_o, l0_norm_mlp, l0_w_up, l0_w_down)

    gy = _conv_pre(x, _row(l1_norm_mix), l1_w_in.astype(BF16), l1_conv_w.astype(F32), _row(l1_conv_b))
    x = mlp(x, gy, l1_w_out, l1_norm_mlp, l1_w_up, l1_w_down)

    w_qv = jnp.concatenate([l2_w_qkv[:, :dm], l2_w_qkv[:, 2 * dm:]], axis=1)
    k, qvt = _sb_proj(x, _row(l2_norm_mix), l2_w_qkv[:, dm:2 * dm].astype(BF16), w_qv.T.astype(BF16),
                      LOG2_E / math.sqrt(dm // SB_HEADS))
    o = _sb_attn(k, qvt, ntri)
    x = mlp(x, o, l2_w_o, l2_norm_mlp, l2_w_up, l2_w_down)

    o = _mla_layer(x, posr, invfc, l3_norm_mix, l3_w_dq, l3_norm_q, l3_w_uq, l3_w_dkv, l3_norm_kv,
                   l3_w_uk, l3_w_uv)
    return mlp(x, o, l3_w_o, l3_norm_mlp, l3_w_up, l3_w_down, final=True)
```

```python
import functools
import math

import jax
import jax.numpy as jnp
from jax import lax
from jax.experimental import pallas as pl
from jax.experimental.pallas import tpu as pltpu

F32 = jnp.float32
BF16 = jnp.bfloat16

EPS = 1e-6
ROPE_THETA = 10000.0
MLA_HEADS = 16
MLA_NOPE = 64
MLA_ROPE = 32
MLA_V = 64
MLA_Q_RANK = 384
MLA_KV_RANK = 256
SB_HEADS = 16

LANES = 128
BF16_ROWS = 16
HEAD_SLAB = LANES
VMEM_LIMIT = 56 * 1024 * 1024

TM_PROJ = 1024
TM_MLP = 1024
TQ_MLA = 2048
TK_MLA = 512
TQ_SB = 1024
TK_SB = 256
LOG2_E = math.log2(math.e)
FF_CHUNK = 1024


def _params(*sem):
    return pltpu.CompilerParams(dimension_semantics=sem, vmem_limit_bytes=VMEM_LIMIT)


def _resident(shape):
    nd = len(shape)
    return pl.BlockSpec(shape, lambda *_: (0,) * nd, pipeline_mode=pl.Buffered(1))


def _rms(xf, g):
    ms = jnp.mean(xf * xf, axis=-1, keepdims=True)
    return xf * lax.rsqrt(ms + EPS) * g


def _dot(a, b):
    return jnp.dot(a, b, preferred_element_type=F32)


def _dot_nt(a, b):
    return lax.dot_general(a, b, (((1,), (1,)), ((), ())), preferred_element_type=F32)


def _mla_proj_kernel(x_ref, posr_ref, g_ref, wd_ref, nq_ref, wuqt_ref, nkv_ref, wuk_ref, wuvt_ref,
                     invfc_ref, qt_ref, k_ref, vt_ref, *, scale):
    h = _rms(x_ref[0], g_ref[...]).astype(BF16)
    d = _dot(h, wd_ref[...])
    cq = _rms(d[:, :MLA_Q_RANK], nq_ref[...]).astype(BF16)
    ckv = _rms(d[:, MLA_Q_RANK:MLA_Q_RANK + MLA_KV_RANK], nkv_ref[...]).astype(BF16)
    kr = d[:, MLA_Q_RANK + MLA_KV_RANK:]

    half = MLA_ROPE // 2
    ang_t = invfc_ref[...] * posr_ref[0].astype(F32)
    cos_t = jnp.cos(ang_t)
    sin_t = jnp.sin(ang_t)

    kr_t = jnp.transpose(kr)
    k1, k2 = kr_t[MLA_NOPE:MLA_NOPE + half], kr_t[MLA_NOPE + half:MLA_NOPE + MLA_ROPE]
    tm = kr.shape[0]
    kr = jnp.transpose(jnp.concatenate(
        [jnp.zeros((MLA_NOPE, tm), F32), k1 * cos_t - k2 * sin_t, k2 * cos_t + k1 * sin_t,
         jnp.zeros((HEAD_SLAB - MLA_NOPE - MLA_ROPE, tm), F32)], axis=0))
    kf = _dot(ckv, wuk_ref[...])
    nope_lanes = lax.broadcasted_iota(jnp.int32, (1, LANES), 1) < MLA_NOPE
    for pair in range(MLA_HEADS // 2):
        both = kf[:, pair * LANES:(pair + 1) * LANES]
        for odd, lanes in enumerate((both, pltpu.roll(both, LANES - MLA_NOPE, 1))):
            hd = 2 * pair + odd
            k_ref[0, :, hd * HEAD_SLAB:(hd + 1) * HEAD_SLAB] = jnp.where(nope_lanes, lanes, kr).astype(BF16)
    vt_ref[0] = _dot_nt(wuvt_ref[...], ckv).astype(BF16)

    qft = _dot_nt(wuqt_ref[...], cq)
    cos_t = cos_t * scale
    sin_t = sin_t * scale
    hd_rows = MLA_NOPE + MLA_ROPE
    for hd in range(MLA_HEADS):
        s0 = hd * hd_rows
        s1, s2, s3 = s0 + MLA_NOPE, s0 + MLA_NOPE + half, s0 + hd_rows
        r0 = hd * HEAD_SLAB
        r1, r2, r3 = r0 + MLA_NOPE, r0 + MLA_NOPE + half, r0 + hd_rows
        x1, x2 = qft[s1:s2], qft[s2:s3]
        qt_ref[0, r0:r1] = (qft[s0:s1] * scale).astype(BF16)
        qt_ref[0, r1:r2] = (x1 * cos_t - x2 * sin_t).astype(BF16)
        qt_ref[0, r2:r3] = (x2 * cos_t + x1 * sin_t).astype(BF16)
        qt_ref[0, r3:r0 + HEAD_SLAB] = jnp.zeros((HEAD_SLAB - hd_rows, qft.shape[1]), BF16)


def _mla_proj(x, posr, g, wd, nq, wuqt, nkv, wuk, wuvt, invfc):
    b, s, dm = x.shape
    tm = TM_PROJ
    hw = MLA_HEADS * HEAD_SLAB
    vw = MLA_HEADS * MLA_V
    tok = lambda w: pl.BlockSpec((1, tm, w), lambda bi, si: (bi, si, 0))
    tok_t = lambda r: pl.BlockSpec((1, r, tm), lambda bi, si: (bi, 0, si))
    return pl.pallas_call(
        functools.partial(_mla_proj_kernel, scale=LOG2_E / math.sqrt(MLA_NOPE + MLA_ROPE)),
        grid=(b, s // tm),
        in_specs=[tok(dm), tok_t(1), _resident(g.shape), _resident(wd.shape), _resident(nq.shape),
                  _resident(wuqt.shape), _resident(nkv.shape), _resident(wuk.shape),
                  _resident(wuvt.shape), _resident(invfc.shape)],
        out_specs=[tok_t(hw), tok(hw), tok_t(vw)],
        out_shape=[jax.ShapeDtypeStruct((b, hw, s), BF16), jax.ShapeDtypeStruct((b, s, hw), BF16),
                   jax.ShapeDtypeStruct((b, vw, s), BF16)],
        compiler_params=_params("arbitrary", "arbitrary"),
        name="mla_proj",
    )(x, posr, g, wd, nq, wuqt, nkv, wuk, wuvt, invfc)


def _mla_attn_kernel(qt_ref, k_ref, vt_ref, o_ref, m_scr, l_scr, acc_scr, sa_scr, sb_scr, *,
                     tq, tk):
    qi = pl.program_id(2)
    m_scr[...] = jnp.full(m_scr.shape, -jnp.inf, F32)
    l_scr[...] = jnp.zeros(l_scr.shape, F32)
    acc_scr[...] = jnp.zeros(acc_scr.shape, F32)

    half = tk // 2

    def scores(j, s_scr, q0=0, diagonal=False):
        ks = pl.multiple_of(j * tk, tk)
        for hd in range(2):
            sl = slice(hd * HEAD_SLAB, (hd + 1) * HEAD_SLAB)
            if diagonal:
                s_scr[hd, :half, q0:q0 + half] = _dot(k_ref[0, pl.ds(ks, half), sl], qt_ref[0, sl, q0:q0 + half])
                s_scr[hd, :, q0 + half:] = _dot(k_ref[0, pl.ds(ks, tk), sl], qt_ref[0, sl, q0 + half:])
            else:
                s_scr[hd, :, q0:] = _dot(k_ref[0, pl.ds(ks, tk), sl], qt_ref[0, sl, q0:])

    def finish(j, s_scr, diag=None):
        ks = pl.multiple_of(j * tk, tk)
        vtb = vt_ref[0, :, pl.ds(ks, tk)]
        if diag is None:
            chunks = [(0, tq, tk, None)]
        else:
            d0 = diag * tk
            chunks = [(d0, half, half, 0), (d0 + half, half, tk, half)]
            chunks += [(c * tk, tk, tk, None) for c in range(diag + 1, tq // tk)]
        ones = jnp.ones((BF16_ROWS, tk), BF16)
        for hd in range(2):
            rows = slice(hd * MLA_V, (hd + 1) * MLA_V)
            vt_ext = jnp.concatenate([vtb[rows, :], ones], axis=0)
            for q0, nq, nk, offset in chunks:
                cols = slice(q0, q0 + nq)
                st = s_scr[hd, :nk, cols]
                if offset is not None:
                    key = lax.broadcasted_iota(jnp.int32, st.shape, 0)
                    qry = lax.broadcasted_iota(jnp.int32, st.shape, 1) + offset
                    st = jnp.where(key <= qry, st, -jnp.inf)
                m_old = m_scr[hd, :, cols]
                m_new = jnp.maximum(m_old, jnp.max(st, axis=0, keepdims=True))
                alpha = jnp.exp2(m_old - m_new)
                pv = _dot(vt_ext[:, :nk], jnp.exp2(st - m_new).astype(BF16))
                l_scr[hd, :, cols] = alpha * l_scr[hd, :, cols] + pv[MLA_V:MLA_V + 1]
                m_scr[hd, :, cols] = m_new
                acc_scr[rows, cols] = acc_scr[rows, cols] * alpha + pv[:MLA_V]

    span = tq // tk
    assert span % 2 == 0
    n_full = span * qi
    bufs = (sa_scr, sb_scr)

    def own_span(then_full_tiles):
        scores(n_full + span - 1, sa_scr, q0=(span - 1) * tk, diagonal=True)
        for i, d in enumerate(reversed(range(span))):
            if d > 0:
                scores(n_full + d - 1, bufs[(i + 1) % 2], q0=(d - 1) * tk, diagonal=True)
            elif then_full_tiles:
                scores(0, bufs[(i + 1) % 2])
            finish(n_full + d, bufs[i % 2], diag=d)

    @pl.when(qi == 0)
    def _():
        own_span(False)

    @pl.when(qi > 0)
    def _():
        own_span(True)

        def body(t, carry):
            scores(2 * t + 1, sb_scr)
            finish(2 * t, sa_scr)
            scores(2 * t + 2, sa_scr)
            finish(2 * t + 1, sb_scr)
            return carry

        lax.fori_loop(0, n_full // 2 - 1, body, 0)
        scores(n_full - 1, sb_scr)
        finish(n_full - 2, sa_scr)
        finish(n_full - 1, sb_scr)

    for hd in range(2):
        rows = slice(hd * MLA_V, (hd + 1) * MLA_V)
        acc_scr[rows, :] = acc_scr[rows, :] * (1.0 / l_scr[hd])
    o_ref[0] = jnp.transpose(acc_scr[...]).astype(BF16)


def _mla_attn(qt, k, vt):
    b, s, _ = k.shape
    tq, tk = TQ_MLA, TK_MLA
    assert s % tq == 0 and tq % tk == 0
    pairs = MLA_HEADS // 2
    return pl.pallas_call(
        functools.partial(_mla_attn_kernel, tq=tq, tk=tk),
        grid=(b, pairs, s // tq),
        in_specs=[pl.BlockSpec((1, 2 * HEAD_SLAB, tq), lambda bi, pi, qi: (bi, pi, qi)),
                  pl.BlockSpec((1, s, 2 * HEAD_SLAB), lambda bi, pi, qi: (bi, 0, pi)),
                  pl.BlockSpec((1, 2 * MLA_V, s), lambda bi, pi, qi: (bi, pi, 0))],
        out_specs=pl.BlockSpec((1, tq, 2 * MLA_V), lambda bi, pi, qi: (bi, qi, pi)),
        out_shape=jax.ShapeDtypeStruct((b, s, MLA_HEADS * MLA_V), BF16),
        scratch_shapes=[pltpu.VMEM((2, 1, tq), F32), pltpu.VMEM((2, 1, tq), F32),
                        pltpu.VMEM((2 * MLA_V, tq), F32),
                        pltpu.VMEM((2, tk, tq), F32), pltpu.VMEM((2, tk, tq), F32)],
        compiler_params=_params("arbitrary", "arbitrary", "arbitrary"),
        name="mla_attn",
    )(qt, k, vt)


def _conv_kernel(x_ref, g_ref, win_ref, cw_ref, cb_ref, o_ref, tail_scr, *, tm, dm):
    @pl.when(pl.program_id(1) == 0)
    def _():
        tail_scr[...] = jnp.zeros(tail_scr.shape, F32)

    h = _rms(x_ref[0], g_ref[...]).astype(BF16)
    bcu = _dot(h, win_ref[...])
    gb = bcu[:, :dm]
    u = bcu[:, dm:2 * dm] * bcu[:, 2 * dm:]
    tail = tail_scr[...]
    row = lax.broadcasted_iota(jnp.int32, (tm, 1), 0)
    prev1 = jnp.where(row == 0, tail[7:8], pltpu.roll(u, 1, 0))
    prev2 = jnp.where(row == 0, tail[6:7], jnp.where(row == 1, tail[7:8], pltpu.roll(u, 2, 0)))
    y = cw_ref[0:1] * prev2 + cw_ref[1:2] * prev1 + cw_ref[2:3] * u + cb_ref[...]
    o_ref[0] = (gb * y).astype(BF16)
    tail_scr[...] = u[tm - 8:, :]


def _conv_pre(x, g, win, cw, cb):
    b, s, dm = x.shape
    tm = TM_PROJ
    tok = pl.BlockSpec((1, tm, dm), lambda bi, si: (bi, si, 0))
    return pl.pallas_call(
        functools.partial(_conv_kernel, tm=tm, dm=dm),
        grid=(b, s // tm),
        in_specs=[tok, _resident(g.shape), _resident(win.shape), _resident(cw.shape),
                  _resident(cb.shape)],
        out_specs=tok,
        out_shape=jax.ShapeDtypeStruct((b, s, dm), BF16),
        scratch_shapes=[pltpu.VMEM((8, dm), F32)],
        compiler_params=_params("arbitrary", "arbitrary"),
        name="conv_pre",
    )(x, g, win, cw, cb)


def _sb_proj_kernel(x_ref, g_ref, wk_ref, wqvt_ref, k_ref, qvt_ref, *, dm, scale):
    h = _rms(x_ref[0], g_ref[...]).astype(BF16)
    k_ref[0] = _dot(h, wk_ref[...]).astype(BF16)
    t = _dot_nt(wqvt_ref[...], h)
    qvt_ref[0, :dm] = (t[:dm] * scale).astype(BF16)
    qvt_ref[0, dm:] = t[dm:].astype(BF16)


def _sb_proj(x, g, wk, wqvt, scale):
    b, s, dm = x.shape
    tm = TM_PROJ
    return pl.pallas_call(
        functools.partial(_sb_proj_kernel, dm=dm, scale=scale),
        grid=(b, s // tm),
        in_specs=[pl.BlockSpec((1, tm, dm), lambda bi, si: (bi, si, 0)), _resident(g.shape),
                  _resident(wk.shape), _resident(wqvt.shape)],
        out_specs=[pl.BlockSpec((1, tm, dm), lambda bi, si: (bi, si, 0)),
                   pl.BlockSpec((1, 2 * dm, tm), lambda bi, si: (bi, 0, si))],
        out_shape=[jax.ShapeDtypeStruct((b, s, dm), BF16), jax.ShapeDtypeStruct((b, 2 * dm, s), BF16)],
        compiler_params=_params("arbitrary", "arbitrary"),
        name="sb_proj",
    )(x, g, wk, wqvt)


def _sb_attn_kernel(qt_ref, k_ref, vt_ref, ntri_ref, o_ref, run_scr, acc_scr, za_scr, zb_scr, qh_scr, lb_scr, *,
                    tq, tk):
    span = tq // tk
    assert span % 2 == 0
    qi = pl.program_id(2)
    last = span * (qi + 1) - 1
    hd_w = LANES // 2
    low = lax.broadcasted_iota(jnp.int32, (LANES, 1), 0) < hd_w
    qt = qt_ref[0]
    zero_q = jnp.zeros_like(qt)
    qh_scr[0] = jnp.where(low, qt, zero_q)
    qh_scr[1] = jnp.where(low, zero_q, qt)
    run_scr[...] = jnp.zeros(run_scr.shape, F32)
    acc_scr[...] = jnp.zeros(acc_scr.shape, F32)

    def scores(u, z_scr, q0=0):
        ks = pl.multiple_of((last - u) * tk, tk)
        kb = k_ref[0, pl.ds(ks, tk), :]
        for hd in range(2):
            z_scr[hd, :, q0:] = _dot(kb, qh_scr[hd, :, q0:])

    def finish(u, z_scr, diag=None):
        ks = pl.multiple_of((last - u) * tk, tk)
        vtb = vt_ref[0, :, pl.ds(ks, tk)]
        q0 = 0 if diag is None else diag * tk
        cols = slice(q0, tq)

        def masses(hd, triangular):
            z = z_scr[hd, :, cols]
            neg_abs = lax.bitcast_convert_type(
                lax.bitcast_convert_type(z, jnp.uint32) | jnp.uint32(0x80000000), F32)
            softplus = jnp.maximum(z, 0.0) + jnp.log2(1.0 + jnp.exp2(neg_abs))
            keep = None
            mass = softplus
            if triangular:
                keep = (lax.broadcasted_iota(jnp.int32, z.shape, 0)
                        < lax.broadcasted_iota(jnp.int32, z.shape, 1))
                mass = jnp.where(keep, softplus, 0.0)
            ntri = ntri_ref[...]
            sums = _dot(ntri, mass.astype(BF16))
            lb_scr[hd, :, cols] = z - softplus
            return keep, sums

        def weights(hd, keep, sums):
            rows = slice(hd * hd_w, (hd + 1) * hd_w)
            a = jnp.exp2(lb_scr[hd, :, cols] + sums[:tk])
            if keep is not None:
                a = jnp.where(keep, a, 0.0)
            run = run_scr[hd, :, cols]
            acc_scr[rows, cols] += _dot(vtb[rows, :], a.astype(BF16)) * jnp.exp2(run)
            run_scr[hd, :, cols] = run + sums[tk:tk + 1]

        staged = [masses(hd, diag is not None) for hd in range(2)]
        for hd in range(2):
            weights(hd, *staged[hd])

    bufs = (za_scr, zb_scr)
    n_full = span * qi

    def own_span(then_full_tiles):
        scores(0, za_scr, q0=(span - 1) * tk)
        for u in range(span):
            if u + 1 < span:
                scores(u + 1, bufs[(u + 1) % 2], q0=(span - 2 - u) * tk)
            elif then_full_tiles:
                scores(u + 1, bufs[(u + 1) % 2])
            finish(u, bufs[u % 2], diag=span - 1 - u)

    @pl.when(qi == 0)
    def _():
        own_span(False)

    @pl.when(qi > 0)
    def _():
        own_span(True)

        def body(t, carry):
            u = span + 2 * t
            scores(u + 1, zb_scr)
            finish(u, za_scr)
            scores(u + 2, za_scr)
            finish(u + 1, zb_scr)
            return carry

        lax.fori_loop(0, n_full // 2 - 1, body, 0)
        u = span + n_full - 2
        scores(u + 1, zb_scr)
        finish(u, za_scr)
        finish(u + 1, zb_scr)

    o_ref[0] = jnp.transpose(acc_scr[...]).astype(BF16)


def _sb_attn(k, qvt, ntri):
    b, s, dm = k.shape
    tq, tk = TQ_SB, TK_SB
    assert s % tq == 0 and tq % tk == 0
    pairs = SB_HEADS // 2
    return pl.pallas_call(
        functools.partial(_sb_attn_kernel, tq=tq, tk=tk),
        grid=(b, pairs, s // tq),
        in_specs=[pl.BlockSpec((1, LANES, tq), lambda bi, pi, qi: (bi, pi, qi)),
                  pl.BlockSpec((1, s, LANES), lambda bi, pi, qi: (bi, 0, pi)),
                  pl.BlockSpec((1, LANES, s), lambda bi, pi, qi: (bi, pairs + pi, 0)),
                  _resident(ntri.shape)],
        out_specs=pl.BlockSpec((1, tq, LANES), lambda bi, pi, qi: (bi, qi, pi)),
        out_shape=jax.ShapeDtypeStruct((b, s, dm), BF16),
        scratch_shapes=[pltpu.VMEM((2, 1, tq), F32), pltpu.VMEM((LANES, tq), F32),
                        pltpu.VMEM((2, tk, tq), F32), pltpu.VMEM((2, tk, tq), F32),
                        pltpu.VMEM((2, LANES, tq), BF16), pltpu.VMEM((2, tk, tq), F32)],
        compiler_params=_params("arbitrary", "arbitrary", "arbitrary"),
        name="sb_attn",
    )(qvt, k, qvt, ntri)


def _mlp_kernel(x_ref, a_ref, wa_ref, g_ref, wup_ref, wdn_ref, gf_ref, o_ref, *, d_ff, final):
    x1 = x_ref[...] + _dot(a_ref[...], wa_ref[...])
    h = _rms(x1, g_ref[...]).astype(BF16)
    acc = x1
    for c in range(d_ff // FF_CHUNK):
        sl = slice(c * FF_CHUNK, (c + 1) * FF_CHUNK)
        r = jnp.maximum(_dot(h, wup_ref[:, sl]), 0.0)
        acc = acc + _dot((r * r).astype(BF16), wdn_ref[sl, :])
    if final:
        acc = _rms(acc, gf_ref[...])
    o_ref[...] = acc


def _mlp(x2, a2, wa, g, wup, wdn, gf, final):
    t, dm = x2.shape
    tm = TM_MLP
    tok = pl.BlockSpec((tm, dm), lambda i: (i, 0))
    return pl.pallas_call(
        functools.partial(_mlp_kernel, d_ff=wup.shape[1], final=final),
        grid=(t // tm,),
        in_specs=[tok, tok, _resident(wa.shape), _resident(g.shape), _resident(wup.shape),
                  _resident(wdn.shape), _resident(gf.shape)],
        out_specs=tok,
        out_shape=jax.ShapeDtypeStruct((t, dm), F32),
        compiler_params=_params("arbitrary"),
        name="out_proj_mlp",
    )(x2, a2, wa, g, wup, wdn, gf)


def _row(v):
    return v.reshape(1, -1).astype(F32)


def _mla_layer(x, posr, invfc, norm_mix, w_dq, norm_q, w_uq, w_dkv, norm_kv, w_uk, w_uv):
    w_kr = jnp.pad(w_dkv[:, MLA_KV_RANK:], ((0, 0), (MLA_NOPE, HEAD_SLAB - MLA_NOPE - MLA_ROPE)))
    wd = jnp.concatenate([w_dq, w_dkv[:, :MLA_KV_RANK], w_kr], axis=1).astype(BF16)
    wuqt = w_uq.T.astype(BF16)
    wuk = w_uk.astype(BF16)
    qt, k, vt = _mla_proj(x, posr, _row(norm_mix), wd, _row(norm_q), wuqt, _row(norm_kv), wuk,
                          w_uv.T.astype(BF16), invfc)
    return _mla_attn(qt, k, vt)


def kernel(x, positions, l0_norm_mix, l0_w_dq, l0_norm_q, l0_w_uq, l0_w_dkv, l0_norm_kv, l0_w_uk, l0_w_uv, l0_w_o, l0_norm_mlp, l0_w_up, l0_w_down, l1_norm_mix, l1_w_in, l1_conv_w, l1_conv_b, l1_w_out, l1_norm_mlp, l1_w_up, l1_w_down, l2_norm_mix, l2_w_qkv, l2_w_o, l2_norm_mlp, l2_w_up, l2_w_down, l3_norm_mix, l3_w_dq, l3_norm_q, l3_w_uq, l3_w_dkv, l3_norm_kv, l3_w_uk, l3_w_uv, l3_w_o, l3_norm_mlp, l3_w_up, l3_w_down, final_norm):
    b, s, dm = x.shape
    t = b * s
    assert s % TM_PROJ == 0
    posr = positions.reshape(b, 1, s)
    inv_freq = ROPE_THETA ** (-jnp.arange(0, MLA_ROPE, 2, dtype=F32) / MLA_ROPE)
    invfc = inv_freq.reshape(-1, 1)
    tri_row = jnp.arange(TK_SB + BF16_ROWS)[:, None]
    ntri = -((jnp.arange(TK_SB)[None, :] > tri_row) | (tri_row >= TK_SB)).astype(BF16)
    gf = _row(final_norm)

    def mlp(xc, a, wa, g, wup, wdn, final=False):
        out = _mlp(xc.reshape(t, dm), a.reshape(t, dm), wa.astype(BF16), _row(g), wup.astype(BF16),
                   wdn.astype(BF16), gf, final)
        return out.reshape(b, s, dm)

    o = _mla_layer(x, posr, invfc, l0_norm_mix, l0_w_dq, l0_norm_q, l0_w_uq, l0_w_dkv, l0_norm_kv,
                   l0_w_uk, l0_w_uv)
    x = mlp(x, o, l0_w_o, l0_norm_mlp, l0_w_up, l0_w_down)

    gy = _conv_pre(x, _row(l1_norm_mix), l1_w_in.astype(BF16), l1_conv_w.astype(F32), _row(l1_conv_b))
    x = mlp(x, gy, l1_w_out, l1_norm_mlp, l1_w_up, l1_w_down)

    w_qv = jnp.concatenate([l2_w_qkv[:, :dm], l2_w_qkv[:, 2 * dm:]], axis=1)
    k, qvt = _sb_proj(x, _row(l2_norm_mix), l2_w_qkv[:, dm:2 * dm].astype(BF16), w_qv.T.astype(BF16),
                      LOG2_E / math.sqrt(dm // SB_HEADS))
    o = _sb_attn(k, qvt, ntri)
    x = mlp(x, o, l2_w_o, l2_norm_mlp, l2_w_up, l2_w_down)

    o = _mla_layer(x, posr, invfc, l3_norm_mix, l3_w_dq, l3_norm_q, l3_w_uq, l3_w_dkv, l3_norm_kv,
                   l3_w_uk, l3_w_uv)
    return mlp(x, o, l3_w_o, l3_norm_mlp, l3_w_up, l3_w_down, final=True)
```
